```python
import jax, jax.numpy as jnp
from jax import lax
import numpy as np

D_MODEL = 2048
BATCH = 8
SEQ = 4096
DEPTH = 4
DEC_BATCH = 16
DEC_SEQ = 16
PAST_LEN = 1024

CHUNK = 64
N_EVEN = (DEPTH + 1) // 2
N_ODD = DEPTH // 2
HA = 8
DH = 128
DA = HA * DH
DB = D_MODEL - DA
KB = 31
DC = D_MODEL
KC = 3
D_FF = -(-8 * D_MODEL // (3 * 256)) * 256
Q_BLOCK = 128
D_IN_AB = 3 * DA + HA + 2 * DB
RMS_EPS = 1e-6
LN_EPS = 1e-5

kernel_name = "fox_conformer_shortconv_adaln_stream_step"


def rms_norm(x, g):
    x32 = x.astype(jnp.float32)
    y = x32 * lax.rsqrt(jnp.mean(x32 * x32, axis=-1, keepdims=True) + RMS_EPS)
    return (y * g.astype(jnp.float32)).astype(x.dtype)


def layer_norm(x, g, b):
    x32 = x.astype(jnp.float32)
    mu = jnp.mean(x32, axis=-1, keepdims=True)
    xc = x32 - mu
    var = jnp.mean(xc * xc, axis=-1, keepdims=True)
    return (xc * lax.rsqrt(var + LN_EPS) * g.astype(jnp.float32) + b.astype(jnp.float32)).astype(x.dtype)


def modulate(x, g, shift, scale):
    return rms_norm(x, g) * (1 + scale[:, None, :]) + shift[:, None, :]


def causal_dwconv(x, hist, w, bias=None):
    xp = jnp.concatenate([hist.astype(x.dtype), x], axis=1)
    y = lax.conv_general_dilated(xp, w[:, None, :].astype(x.dtype), window_strides=(1,), padding='VALID',
                                 dimension_numbers=('NWC', 'WIO', 'NWC'), feature_group_count=x.shape[-1])
    if bias is not None:
        y = y + bias
    new_hist = xp[:, -(w.shape[0] - 1):]
    return y, new_hist


def fox_prompt(q, k, v, logf):
    B, T, H, Dh = q.shape
    L = jnp.cumsum(logf.astype(jnp.float32), axis=1).transpose(0, 2, 1)
    nb = T // Q_BLOCK
    qb = q.reshape(B, nb, Q_BLOCK, H, Dh).transpose(1, 0, 2, 3, 4)
    Lq = L.reshape(B, H, nb, Q_BLOCK).transpose(2, 0, 1, 3)
    kpos = jnp.arange(T)
    scale = Dh ** -0.5

    def block(args):
        i, qi, Lqi = args
        s = jnp.einsum('bqhd,bkhd->bhqk', qi, k, preferred_element_type=jnp.float32) * scale
        s = s + Lqi[..., :, None] - L[..., None, :]
        qpos = i * Q_BLOCK + jnp.arange(Q_BLOCK)
        s = jnp.where(kpos[None, :] <= qpos[:, None], s, -jnp.inf)
        p = jax.nn.softmax(s, axis=-1)
        return jnp.einsum('bhqk,bkhd->bqhd', p.astype(v.dtype), v)

    out = lax.map(block, (jnp.arange(nb), qb, Lq))
    return out.transpose(1, 0, 2, 3, 4).reshape(B, T, H * Dh)


def fox_sample(q, k_all, v_all, logf_all, past):
    B, S, H, Dh = q.shape
    Tk = k_all.shape[1]
    L = jnp.cumsum(logf_all.astype(jnp.float32), axis=1).transpose(0, 2, 1)
    s = jnp.einsum('bqhd,bkhd->bhqk', q, k_all, preferred_element_type=jnp.float32) * (Dh ** -0.5)
    s = s + L[..., past:, None] - L[..., None, :]
    qpos = past + jnp.arange(S)
    s = jnp.where(jnp.arange(Tk)[None, :] <= qpos[:, None], s, -jnp.inf)
    p = jax.nn.softmax(s, axis=-1)
    out = jnp.einsum('bhqk,bkhd->bqhd', p.astype(v_all.dtype), v_all)
    return out.reshape(B, S, H * Dh)


def mixer_ab(h, kv_cache, convb_hist, w_in, b_f, dw_w, dw_b, ln_g, ln_b, w_out):
    B, T, _ = h.shape
    proj = h @ w_in
    q, k, v, fz, ga, gb = jnp.split(proj, [DA, 2 * DA, 3 * DA, 3 * DA + HA, 3 * DA + HA + DB], axis=-1)
    q = q.reshape(B, T, HA, DH)
    k = k.reshape(B, T, HA, DH)
    v = v.reshape(B, T, HA, DH)
    logf = jax.nn.log_sigmoid((fz + b_f).astype(jnp.float32))
    if kv_cache is None:
        att = fox_prompt(q, k, v, logf)
    else:
        ck, cv, cf = kv_cache
        att = fox_sample(q, jnp.concatenate([ck.astype(k.dtype), k], axis=1),
                         jnp.concatenate([cv.astype(v.dtype), v], axis=1),
                         jnp.concatenate([cf.astype(jnp.float32), logf], axis=1), ck.shape[1])
    u = ga * jax.nn.sigmoid(gb)
    uc, new_convb = causal_dwconv(u, convb_hist, dw_w, dw_b)
    z = jax.nn.silu(layer_norm(uc, ln_g, ln_b))
    y = jnp.concatenate([att.astype(h.dtype), z], axis=-1) @ w_out
    return y, k, v, logf, new_convb


def mixer_c(h, convc_hist, w_in, conv_w, w_out):
    bg, cg, xv = jnp.split(h @ w_in, 3, axis=-1)
    uc, new_hist = causal_dwconv(cg * xv, convc_hist, conv_w)
    return (bg * uc) @ w_out, new_hist


def swiglu(h, w_gate, w_up, w_down):
    return (jax.nn.silu(h @ w_gate) * (h @ w_up)) @ w_down


def trunk(x, c, p, cache):
    B = x.shape[0]
    ks, vs, fs, bs, cs = [], [], [], [], []
    for l in range(DEPTH):
        mod = jax.nn.silu(c) @ p['ada_w'][l] + p['ada_b'][l]
        sh1, sc1, g1, sh2, sc2, g2 = jnp.split(mod, 6, axis=-1)
        h = modulate(x, p['norm_mix_g'][l], sh1, sc1)
        e = l // 2
        if l % 2 == 0:
            if cache is None:
                kv = None
                bh = jnp.zeros((B, KB - 1, DB), x.dtype)
            else:
                kv = (cache['k'][e], cache['v'][e], cache['logf'][e])
                bh = cache['convb'][e]
            y, k, v, f, bnew = mixer_ab(h, kv, bh, p['w_in_ab'][e], p['b_f'][e], p['dw_b_w'][e],
                                        p['dw_b_bias'][e], p['ln_b_g'][e], p['ln_b_b'][e], p['w_out_ab'][e])
            ks.append(k); vs.append(v); fs.append(f); bs.append(bnew)
        else:
            chist = jnp.zeros((B, KC - 1, DC), x.dtype) if cache is None else cache['convc'][e]
            y, cnew = mixer_c(h, chist, p['w_in_c'][e], p['conv_c_w'][e], p['w_out_c'][e])
            cs.append(cnew)
        x = x + g1[:, None, :] * y
        h = modulate(x, p['norm_ffn_g'][l], sh2, sc2)
        x = x + g2[:, None, :] * swiglu(h, p['w_gate'][l], p['w_up'][l], p['w_down'][l])
    y_out = rms_norm(x, p['final_g'])
    return (y_out, jnp.stack(ks), jnp.stack(vs), jnp.stack(fs), jnp.stack(bs), jnp.stack(cs))


def setup_inputs(seed: int = 0) -> dict:
    key = jax.random.key(seed)
    ks = jax.random.split(key, 32)
    f32 = jnp.float32
    nrm = lambda k, shape, s=1.0: (jax.random.normal(k, shape, f32) * s)
    D = D_MODEL
    return {
        'x_prompt': nrm(ks[0], (BATCH, SEQ, D)),
        'x_sample': nrm(ks[1], (DEC_BATCH, DEC_SEQ, D)),
        'cache_k': nrm(ks[2], (N_EVEN, DEC_BATCH, PAST_LEN, HA, DH)),
        'cache_v': nrm(ks[3], (N_EVEN, DEC_BATCH, PAST_LEN, HA, DH)),
        'cache_logf': jax.nn.log_sigmoid(3.0 + nrm(ks[4], (N_EVEN, DEC_BATCH, PAST_LEN, HA))),
        'state_convb': nrm(ks[5], (N_EVEN, DEC_BATCH, KB - 1, DB), 0.5),
        'state_convc': nrm(ks[6], (N_ODD, DEC_BATCH, KC - 1, DC), 0.5),
        'c_prompt': nrm(ks[7], (BATCH, D)),
        'c_sample': nrm(ks[8], (DEC_BATCH, D)),
        'ada_w': nrm(ks[9], (DEPTH, D, 6 * D), 0.5 * D ** -0.5),
        'ada_b': nrm(ks[10], (DEPTH, 6 * D), 0.02),
        'norm_mix_g': 1.0 + nrm(ks[11], (DEPTH, D), 0.02),
        'norm_ffn_g': 1.0 + nrm(ks[12], (DEPTH, D), 0.02),
        'w_in_ab': nrm(ks[13], (N_EVEN, D, D_IN_AB), D ** -0.5),
        'b_f': 3.0 + nrm(ks[14], (N_EVEN, HA), 0.5),
        'dw_b_w': nrm(ks[15], (N_EVEN, KB, DB), KB ** -0.5),
        'dw_b_bias': nrm(ks[16], (N_EVEN, DB), 0.02),
        'ln_b_g': 1.0 + nrm(ks[17], (N_EVEN, DB), 0.02),
        'ln_b_b': nrm(ks[18], (N_EVEN, DB), 0.02),
        'w_out_ab': nrm(ks[19], (N_EVEN, DA + DB, D), (DA + DB) ** -0.5),
        'w_in_c': nrm(ks[20], (N_ODD, D, 3 * DC), D ** -0.5),
        'conv_c_w': nrm(ks[21], (N_ODD, KC, DC), KC ** -0.5),
        'w_out_c': nrm(ks[22], (N_ODD, DC, D), DC ** -0.5),
        'w_gate': nrm(ks[23], (DEPTH, D, D_FF), D ** -0.5),
        'w_up': nrm(ks[24], (DEPTH, D, D_FF), D ** -0.5),
        'w_down': nrm(ks[25], (DEPTH, D_FF, D), D_FF ** -0.5),
        'final_g': 1.0 + nrm(ks[26], (D,), 0.02),
    }


def reference(x_prompt, x_sample, cache_k, cache_v, cache_logf, state_convb, state_convc, c_prompt, c_sample,
              ada_w, ada_b, norm_mix_g, norm_ffn_g, w_in_ab, b_f, dw_b_w, dw_b_bias, ln_b_g, ln_b_b, w_out_ab,
              w_in_c, conv_c_w, w_out_c, w_gate, w_up, w_down, final_g):
    p = {'ada_w': ada_w, 'ada_b': ada_b, 'norm_mix_g': norm_mix_g, 'norm_ffn_g': norm_ffn_g,
         'w_in_ab': w_in_ab, 'b_f': b_f, 'dw_b_w': dw_b_w, 'dw_b_bias': dw_b_bias, 'ln_b_g': ln_b_g,
         'ln_b_b': ln_b_b, 'w_out_ab': w_out_ab, 'w_in_c': w_in_c, 'conv_c_w': conv_c_w, 'w_out_c': w_out_c,
         'w_gate': w_gate, 'w_up': w_up, 'w_down': w_down, 'final_g': final_g}
    cache = {'k': cache_k, 'v': cache_v, 'logf': cache_logf, 'convb': state_convb, 'convc': state_convc}
    y_prompt, k_p, v_p, logf_p, convb_p, convc_p = trunk(x_prompt, c_prompt, p, None)
    y_sample, k_s, v_s, logf_s, convb_s, convc_s = trunk(x_sample, c_sample, p, cache)
    return (y_prompt, y_sample, k_p, v_p, logf_p, convb_p, convc_p, k_s, v_s, logf_s, convb_s, convc_s)
```

```python
import functools

import jax
import jax.numpy as jnp
from jax import lax
from jax.experimental import pallas as pl
from jax.experimental.pallas import tpu as pltpu

RMS_EPS = 1e-6
LN_EPS = 1e-5

V7X_VMEM_BYTES = 64 * 1024 * 1024
VMEM_LIMIT = V7X_VMEM_BYTES - 8 * 1024 * 1024
LANES = 128
SUBLANES = 8
CONV_HALO = 32

F32 = jnp.float32
BF16 = jnp.bfloat16

SH1, SC1, G1, SH2, SC2, G2 = range(6)


def _cparams(n_axes):
    return pltpu.CompilerParams(dimension_semantics=("arbitrary",) * n_axes, vmem_limit_bytes=VMEM_LIMIT)


def _resident(shape):
    nd = len(shape)
    return pl.BlockSpec(shape, lambda *_: (0,) * nd, pipeline_mode=pl.Buffered(1))


def _mod_spec(mod, col, tiles_per_group, d):
    mr = mod.shape[1]
    return pl.BlockSpec((None, mr, d), lambda i, *_: (i // tiles_per_group, 0, col))


def _sigmoid(x):
    return 1.0 / (1.0 + jnp.exp(-x))


def _silu(x):
    return x * _sigmoid(x)


def _modulate(x, g, shift, scale):
    ms = jnp.mean(x * x, axis=-1, keepdims=True)
    y = (x * lax.rsqrt(ms + RMS_EPS)) * g
    return y * (1.0 + scale) + shift


def _dot(a, b):
    return jnp.dot(a, b, preferred_element_type=F32)


def _dot_nt(a, b):
    return lax.dot_general(a, b, (((1,), (1,)), ((), ())), preferred_element_type=F32)


def _ada_kernel(c_ref, w_ref, b_ref, o_ref):
    a = _silu(c_ref[...]).astype(BF16)
    o_ref[...] = _dot(a, w_ref[...].astype(BF16)) + b_ref[...]


def _ada(c_all, ada_w, ada_b):
    depth, d, n = ada_w.shape
    r = c_all.shape[0]
    tn = 1024 if n % 1024 == 0 else n
    return pl.pallas_call(
        _ada_kernel,
        grid=(depth, n // tn),
        in_specs=[
            pl.BlockSpec((r, d), lambda l, j: (0, 0)),
            pl.BlockSpec((None, d, tn), lambda l, j: (l, 0, j)),
            pl.BlockSpec((None, 1, tn), lambda l, j: (l, 0, j)),
        ],
        out_specs=pl.BlockSpec((None, r, tn), lambda l, j: (l, 0, j)),
        out_shape=jax.ShapeDtypeStruct((depth, r, n), F32),
        compiler_params=_cparams(2),
        name="ada",
    )(c_all, ada_w, ada_b.reshape(depth, 1, n))


def _in_ab_kernel(x_ref, sh_ref, sc_ref, g_ref, wqkv_ref, wf_ref, bf_ref, wg_ref,
                  q_ref, k32_ref, v32_ref, kbf_ref, vbf_ref, u_ref, lf_ref, *, da, db, qscale):
    h = _modulate(x_ref[...], g_ref[...], sh_ref[...], sc_ref[...]).astype(BF16)
    q_ref[...] = (_dot(h, wqkv_ref[:, 0:da]) * qscale).astype(BF16)
    k = _dot(h, wqkv_ref[:, da:2 * da])
    k32_ref[...] = k
    kbf_ref[...] = k.astype(BF16)
    v = _dot(h, wqkv_ref[:, 2 * da:3 * da])
    v32_ref[...] = v
    vbf_ref[...] = v.astype(BF16)
    ga = _dot(h, wg_ref[:, 0:db])
    gb = _dot(h, wg_ref[:, db:2 * db])
    u_ref[...] = ga * _sigmoid(gb)
    fz = _dot_nt(wf_ref[...], h) + bf_ref[...]
    lf_ref[...] = jnp.minimum(fz, 0.0) - jnp.log1p(jnp.exp(-jnp.abs(fz)))


def _in_ab(x, mod, tiles_per_group, tm, g, wqkv, wf, bf, wg, dh):
    r, d = x.shape
    da = wqkv.shape[1] // 3
    db = wg.shape[1] // 2
    ha = wf.shape[0]
    row = lambda w: pl.BlockSpec((tm, w), lambda i: (i, 0))
    return pl.pallas_call(
        functools.partial(_in_ab_kernel, da=da, db=db, qscale=float(dh) ** -0.5),
        grid=(r // tm,),
        in_specs=[
            row(d),
            _mod_spec(mod, SH1, tiles_per_group, d),
            _mod_spec(mod, SC1, tiles_per_group, d),
            _resident((1, d)),
            _resident(wqkv.shape),
            _resident(wf.shape),
            _resident(bf.shape),
            _resident(wg.shape),
        ],
        out_specs=[row(da), row(da), row(da), row(da), row(da), row(db),
                   pl.BlockSpec((ha, tm), lambda i: (0, i))],
        out_shape=[
            jax.ShapeDtypeStruct((r, da), BF16),
            jax.ShapeDtypeStruct((r, da), F32),
            jax.ShapeDtypeStruct((r, da), F32),
            jax.ShapeDtypeStruct((r, da), BF16),
            jax.ShapeDtypeStruct((r, da), BF16),
            jax.ShapeDtypeStruct((r, db), F32),
            jax.ShapeDtypeStruct((ha, r), F32),
        ],
        compiler_params=_cparams(1),
        name="in_ab",
    )(x, mod, mod, g, wqkv, wf, bf, wg)


def _cumsum_kernel(x_ref, o_ref):
    x = x_ref[...]
    t = x.shape[-1]
    lane = lax.broadcasted_iota(jnp.int32, x.shape, 1)
    s = 1
    while s < t:
        x = x + jnp.where(lane >= s, pltpu.roll(x, s, axis=1), 0.0)
        s *= 2
    o_ref[...] = x


def _cumsum(logf_t, nb, t):
    ha = logf_t.shape[0]
    return pl.pallas_call(
        _cumsum_kernel,
        grid=(nb,),
        in_specs=[pl.BlockSpec((ha, t), lambda b: (0, b))],
        out_specs=pl.BlockSpec((None, ha, t), lambda b: (b, 0, 0)),
        out_shape=jax.ShapeDtypeStruct((nb, ha, t), F32),
        compiler_params=_cparams(1),
        name="cumsum",
    )(logf_t)


def _attn_prompt_kernel(q_ref, k_ref, v_ref, l_ref, o_ref, m_scr, s_scr, acc_scr, *, blk, nblk):
    row = lax.broadcasted_iota(jnp.int32, (blk, blk), 0)
    col = lax.broadcasted_iota(jnp.int32, (blk, blk), 1)

    def update(q, j, masked):
        c0 = pl.multiple_of(j * blk, blk)
        s = _dot_nt(q, k_ref[pl.ds(c0, blk), :]) - l_ref[j]
        if masked:
            s = jnp.where(col <= row, s, -jnp.inf)
        m_old = m_scr[...]
        m_new = jnp.maximum(m_old, jnp.max(s, axis=-1, keepdims=True))
        p = jnp.exp(s - m_new)
        alpha = jnp.exp(m_old - m_new)
        s_scr[...] = alpha * s_scr[...] + jnp.sum(p, axis=-1, keepdims=True)
        acc_scr[...] = alpha * acc_scr[...] + _dot(p.astype(BF16), v_ref[pl.ds(c0, blk), :])
        m_scr[...] = m_new

    def q_block(i, carry):
        r0 = pl.multiple_of(i * blk, blk)
        q = q_ref[pl.ds(r0, blk), :]
        m_scr[...] = jnp.full(m_scr.shape, -jnp.inf, F32)
        s_scr[...] = jnp.zeros(s_scr.shape, F32)
        acc_scr[...] = jnp.zeros(acc_scr.shape, F32)

        def kv_block(j, c):
            update(q, j, masked=False)
            return c

        lax.fori_loop(0, i, kv_block, 0)
        update(q, i, masked=True)
        o_ref[pl.ds(r0, blk), :] = (acc_scr[...] / s_scr[...]).astype(o_ref.dtype)
        return carry

    lax.fori_loop(0, nblk, q_block, 0)


def _attn_prompt(q, k, v, cum, b, t, ha, dh, blk):
    nblk = t // blk
    cum5 = cum.reshape(b, ha, nblk, 1, blk)
    head = pl.BlockSpec((t, dh), lambda bi, h: (bi, h))
    return pl.pallas_call(
        functools.partial(_attn_prompt_kernel, blk=blk, nblk=nblk),
        grid=(b, ha),
        in_specs=[head, head, head,
                  pl.BlockSpec((None, None, nblk, 1, blk), lambda bi, h: (bi, h, 0, 0, 0))],
        out_specs=head,
        out_shape=jax.ShapeDtypeStruct((b * t, ha * dh), BF16),
        scratch_shapes=[pltpu.VMEM((blk, 1), F32), pltpu.VMEM((blk, 1), F32), pltpu.VMEM((blk, dh), F32)],
        compiler_params=_cparams(2),
        name="attn_prompt",
    )(q, k, v, cum5)


def _attn_sample_kernel(q_ref, kn_ref, vn_ref, ck_ref, cv_ref, l_ref, o_ref, *, ha, dh, past, npad):
    s_new = q_ref.shape[0]
    row = lax.broadcasted_iota(jnp.int32, (s_new, npad), 0)
    col = lax.broadcasted_iota(jnp.int32, (s_new, npad), 1)
    for h in range(ha):
        hs = slice(h * dh, (h + 1) * dh)
        q = q_ref[:, hs]
        sc = _dot_nt(q, ck_ref[:, hs].astype(BF16)) - l_ref[h:h + 1, 0:past]
        sn = _dot_nt(q, kn_ref[:, hs]) - l_ref[h:h + 1, past:past + npad]
        sn = jnp.where(col <= row, sn, -jnp.inf)
        m = jnp.maximum(jnp.max(sc, axis=-1, keepdims=True), jnp.max(sn, axis=-1, keepdims=True))
        pc = jnp.exp(sc - m)
        pn = jnp.exp(sn - m)
        denom = jnp.sum(pc, axis=-1, keepdims=True) + jnp.sum(pn, axis=-1, keepdims=True)
        acc = _dot(pc.astype(BF16), cv_ref[:, hs].astype(BF16)) + _dot(pn.astype(BF16), vn_ref[:, hs])
        o_ref[:, hs] = (acc / denom).astype(o_ref.dtype)


def _attn_sample(q, kn, vn, cache_k, cache_v, layer, cum, ha, dh):
    b, s, da = q.shape
    npad = kn.shape[1]
    past = cache_k.shape[2]
    tp = cum.shape[-1]
    per_b = lambda rows: pl.BlockSpec((None, rows, da), lambda bi: (bi, 0, 0))
    cache = pl.BlockSpec((None, None, past, da), lambda bi: (layer, bi, 0, 0))
    return pl.pallas_call(
        functools.partial(_attn_sample_kernel, ha=ha, dh=dh, past=past, npad=npad),
        grid=(b,),
        in_specs=[per_b(s), per_b(npad), per_b(npad), cache, cache,
                  pl.BlockSpec((None, ha, tp), lambda bi: (bi, 0, 0))],
        out_specs=per_b(s),
        out_shape=jax.ShapeDtypeStruct((b, s, da), BF16),
        compiler_params=_cparams(1),
        name="attn_sample",
    )(q, kn, vn, cache_k, cache_v, cum)


def _convb_kernel(u_ref, hist_ref, w_ref, b_ref, g_ref, beta_ref, z_ref, ext_scr, y_scr,
                  *, kb, tm, rc, tiles_per_seq):
    i = pl.program_id(1)
    c = u_ref.shape[-1]

    @pl.when(i == 0)
    def _():
        ext_scr[0:CONV_HALO, :] = hist_ref[...]

    ext_scr[CONV_HALO:CONV_HALO + tm, :] = u_ref[...]
    off = CONV_HALO - (kb - 1)
    for r0 in range(0, tm, rc):
        for c0 in range(0, c, LANES):
            acc = jnp.zeros((rc, LANES), F32)
            for k in range(kb):
                acc = acc + w_ref[k:k + 1, c0:c0 + LANES] * ext_scr[r0 + off + k:r0 + off + k + rc, c0:c0 + LANES]
            y_scr[r0:r0 + rc, c0:c0 + LANES] = acc
    y = y_scr[...] + b_ref[...]
    mu = jnp.mean(y, axis=-1, keepdims=True)
    yc = y - mu
    var = jnp.mean(yc * yc, axis=-1, keepdims=True)
    z = yc * lax.rsqrt(var + LN_EPS) * g_ref[...] + beta_ref[...]
    z_ref[...] = _silu(z).astype(z_ref.dtype)
    if tiles_per_seq > 1:
        ext_scr[0:CONV_HALO, :] = ext_scr[tm:tm + CONV_HALO, :]


def _convb(u, hist_pad, w, bias, g, beta, tm):
    b, t, c = u.shape
    kb = w.shape[0]
    assert kb - 1 <= CONV_HALO and t % tm == 0 and (tm >= CONV_HALO or t == tm)
    rc = min(tm, 64)
    return pl.pallas_call(
        functools.partial(_convb_kernel, kb=kb, tm=tm, rc=rc, tiles_per_seq=t // tm),
        grid=(b, t // tm),
        in_specs=[
            pl.BlockSpec((None, tm, c), lambda bi, i: (bi, i, 0)),
            pl.BlockSpec((None, CONV_HALO, c), lambda bi, i: (bi, 0, 0)),
            _resident(w.shape), _resident((1, c)), _resident((1, c)), _resident((1, c)),
        ],
        out_specs=pl.BlockSpec((None, tm, c), lambda bi, i: (bi, i, 0)),
        out_shape=jax.ShapeDtypeStruct((b, t, c), BF16),
        scratch_shapes=[pltpu.VMEM((CONV_HALO + tm, c), F32), pltpu.VMEM((tm, c), F32)],
        compiler_params=_cparams(2),
        name="convb",
    )(u, hist_pad, w, bias, g, beta)


def _in_c_kernel(x_ref, sh_ref, sc_ref, g_ref, w_ref, cw_ref, h0_ref, h1_ref, o_ref, st_ref, ext_scr,
                 *, dc, tm, tiles_per_seq, seg):
    h = _modulate(x_ref[...], g_ref[...], sh_ref[...], sc_ref[...]).astype(BF16)
    cx = _dot(h, w_ref[:, dc:2 * dc]) * _dot(h, w_ref[:, 2 * dc:3 * dc])
    if seg is None:
        i = pl.program_id(0) % tiles_per_seq

        @pl.when(i == 0)
        def _():
            ext_scr[SUBLANES - 2:SUBLANES - 1, :] = h0_ref[...]
            ext_scr[SUBLANES - 1:SUBLANES, :] = h1_ref[...]
    else:
        ext_scr[0:SUBLANES, :] = jnp.zeros((SUBLANES, dc), F32)

    ext_scr[SUBLANES:SUBLANES + tm, :] = cx
    m1 = ext_scr[SUBLANES - 1:SUBLANES - 1 + tm, :]
    m2 = ext_scr[SUBLANES - 2:SUBLANES - 2 + tm, :]
    if seg is not None:
        pos = lax.broadcasted_iota(jnp.int32, (tm, 1), 0) % seg
        m1 = jnp.where(pos == 0, h1_ref[...], m1)
        m2 = jnp.where(pos == 0, h0_ref[...], jnp.where(pos == 1, h1_ref[...], m2))
    uc = cw_ref[0:1, :] * m2 + cw_ref[1:2, :] * m1 + cw_ref[2:3, :] * cx
    o_ref[...] = (_dot(h, w_ref[:, 0:dc]) * uc).astype(o_ref.dtype)
    if seg is None:
        st_ref[...] = ext_scr[tm:tm + SUBLANES, :]
        if tiles_per_seq > 1:
            ext_scr[0:SUBLANES, :] = ext_scr[tm:tm + SUBLANES, :]
    else:
        st_ref[...] = cx


def _in_c(x, mod, tiles_per_group, tm, g, w, cw, h0, h1, tiles_per_seq, seg):
    r, d = x.shape
    dc = w.shape[1] // 3
    row = lambda width: pl.BlockSpec((tm, width), lambda i: (i, 0))
    if seg is None:
        nseq = r // (tm * tiles_per_seq)
        hspec = pl.BlockSpec((None, 1, dc), lambda i: (i // tiles_per_seq, 0, 0))
        st_spec = pl.BlockSpec((None, SUBLANES, dc), lambda i: (i // tiles_per_seq, 0, 0))
        st_shape = jax.ShapeDtypeStruct((nseq, SUBLANES, dc), F32)
    else:
        hspec = row(dc)
        st_spec = row(dc)
        st_shape = jax.ShapeDtypeStruct((r, dc), F32)
    return pl.pallas_call(
        functools.partial(_in_c_kernel, dc=dc, tm=tm, tiles_per_seq=tiles_per_seq, seg=seg),
        grid=(r // tm,),
        in_specs=[
            row(d),
            _mod_spec(mod, SH1, tiles_per_group, d),
            _mod_spec(mod, SC1, tiles_per_group, d),
            _resident((1, d)), _resident(w.shape), _resident(cw.shape), hspec, hspec,
        ],
        out_specs=[row(dc), st_spec],
        out_shape=[jax.ShapeDtypeStruct((r, dc), BF16), st_shape],
        scratch_shapes=[pltpu.VMEM((SUBLANES + tm, dc), F32)],
        compiler_params=_cparams(1),
        name="in_c",
    )(x, mod, mod, g, w, cw, h0, h1)


def _out_kernel(*refs, n_in):
    a_refs, w_refs = refs[:n_in], refs[n_in:2 * n_in]
    x_ref, gate_ref, o_ref = refs[2 * n_in:]
    y = _dot(a_refs[0][...], w_refs[0][...])
    for a_ref, w_ref in zip(a_refs[1:], w_refs[1:]):
        y = y + _dot(a_ref[...], w_ref[...])
    o_ref[...] = x_ref[...] + gate_ref[...] * y


def _out_proj(acts, weights, x, mod, tiles_per_group, tm):
    r, d = x.shape
    row = lambda w: pl.BlockSpec((tm, w), lambda i: (i, 0))
    return pl.pallas_call(
        functools.partial(_out_kernel, n_in=len(acts)),
        grid=(r // tm,),
        in_specs=[row(a.shape[1]) for a in acts] + [_resident(w.shape) for w in weights]
        + [row(d), _mod_spec(mod, G1, tiles_per_group, d)],
        out_specs=row(d),
        out_shape=jax.ShapeDtypeStruct((r, d), F32),
        compiler_params=_cparams(1),
        name="out_proj",
    )(*acts, *weights, x, mod)


def _ffn_kernel(x_ref, sh_ref, sc_ref, gate_ref, g_ref, wg_ref, wu_ref, wd_ref, fg_ref, o_ref, h_scr,
                *, final):
    j = pl.program_id(1)

    @pl.when(j == 0)
    def _():
        h_scr[...] = _modulate(x_ref[...], g_ref[...], sh_ref[...], sc_ref[...]).astype(BF16)

    h = h_scr[...]
    act = (_silu(_dot(h, wg_ref[...])) * _dot(h, wu_ref[...])).astype(BF16)
    d = _dot(act, wd_ref[...])

    @pl.when(j == 0)
    def _():
        o_ref[...] = d

    @pl.when(j > 0)
    def _():
        o_ref[...] += d

    @pl.when(j == pl.num_programs(1) - 1)
    def _():
        xn = x_ref[...] + gate_ref[...] * o_ref[...]
        if final:
            ms = jnp.mean(xn * xn, axis=-1, keepdims=True)
            xn = (xn * lax.rsqrt(ms + RMS_EPS)) * fg_ref[...]
        o_ref[...] = xn


def _ffn(x, mod, tiles_per_group, tm, tf, g, wg, wu, wd, final_g, final):
    r, d = x.shape
    f = wg.shape[1]
    row = pl.BlockSpec((tm, d), lambda i, j: (i, 0))
    return pl.pallas_call(
        functools.partial(_ffn_kernel, final=final),
        grid=(r // tm, f // tf),
        in_specs=[
            row,
            _mod_spec(mod, SH2, tiles_per_group, d),
            _mod_spec(mod, SC2, tiles_per_group, d),
            _mod_spec(mod, G2, tiles_per_group, d),
            _resident((1, d)),
            pl.BlockSpec((d, tf), lambda i, j: (0, j)),
            pl.BlockSpec((d, tf), lambda i, j: (0, j)),
            pl.BlockSpec((tf, d), lambda i, j: (j, 0)),
            _resident((1, d)),
        ],
        out_specs=row,
        out_shape=jax.ShapeDtypeStruct((r, d), F32),
        scratch_shapes=[pltpu.VMEM((tm, d), BF16)],
        compiler_params=_cparams(2),
        name="ffn",
    )(x, mod, mod, mod, g, wg, wu, wd, final_g)


def _row_tile(t, cap):
    tm = min(t, cap)
    while t % tm:
        tm -= SUBLANES
    return tm


def _ff_tile(f, cap=512):
    tf = min(f, cap)
    while f % tf or tf % LANES:
        tf -= LANES
    return tf


def _trunk(x3, mods, p, cache):
    b, t, d = x3.shape
    depth = mods.shape[0]
    ha, dh, da, db = p["ha"], p["dh"], p["da"], p["db"]
    kb, kc = p["dw_b_w"].shape[1], p["conv_c_w"].shape[1]
    r = b * t
    x = x3.reshape(r, d)
    if cache is None:
        tm = _row_tile(t, 512)
        tiles_per_group = t // tm
        tiles_per_seq, seg = t // tm, None
    else:
        tm = r
        tiles_per_group = 1
        tiles_per_seq, seg = 1, t
    ks, vs, fs, bs, cs = [], [], [], [], []
    for l in range(depth):
        mod = mods[l]
        e = l // 2
        if l % 2 == 0:
            q, k32, v32, kbf, vbf, u, lf_t = _in_ab(x, mod, tiles_per_group, tm, p["norm_mix_g"][l],
                                                    p["wqkv"][e], p["wf"][e], p["bf"][e], p["wg_ab"][e], dh)
            u3 = u.reshape(b, t, db)
            if cache is None:
                cum = _cumsum(lf_t, b, t)
                att = _attn_prompt(q, kbf, vbf, cum, b, t, ha, dh, _row_tile(t, 512))
                hist = jnp.zeros((b, kb - 1, db), F32)
            else:
                past = cache["k"].shape[2]
                npad = LANES
                lf_new = lf_t.reshape(ha, b, t).transpose(1, 0, 2)
                lf_all = jnp.concatenate([cache["logf"][e].transpose(0, 2, 1), lf_new,
                                          jnp.zeros((b, ha, npad - t), F32)], axis=-1)
                cum = _cumsum(lf_all.transpose(1, 0, 2).reshape(ha, b * (past + npad)), b, past + npad)
                pad = lambda a: jnp.pad(a.reshape(b, t, da), ((0, 0), (0, npad - t), (0, 0)))
                att = _attn_sample(q.reshape(b, t, da), pad(kbf), pad(vbf), cache["k"], cache["v"], e,
                                   cum, ha, dh).reshape(r, da)
                hist = cache["convb"][e]
            hist_pad = jnp.pad(hist, ((0, 0), (CONV_HALO - (kb - 1), 0), (0, 0)))
            z = _convb(u3, hist_pad, p["dw_b_w"][e], p["dw_b_bias"][e], p["ln_b_g"][e], p["ln_b_b"][e],
                       _row_tile(t, 256)).reshape(r, db)
            x = _out_proj([att, z], [p["wout_a"][e], p["wout_b"][e]], x, mod, tiles_per_group, tm)
            ks.append(k32.reshape(b, t, ha, dh))
            vs.append(v32.reshape(b, t, ha, dh))
            fs.append(lf_t.reshape(ha, b, t).transpose(1, 2, 0))
            bs.append(jnp.concatenate([hist, u3], axis=1)[:, -(kb - 1):])
        else:
            hist = jnp.zeros((b, kc - 1, d), F32) if cache is None else cache["convc"][e]
            if seg is None:
                h0, h1 = hist[:, 0:1], hist[:, 1:2]
            else:
                h0, h1 = jnp.repeat(hist[:, 0], t, axis=0), jnp.repeat(hist[:, 1], t, axis=0)
            gated, st = _in_c(x, mod, tiles_per_group, tm, p["norm_mix_g"][l], p["w_in_c"][e],
                              p["conv_c_w"][e], h0, h1, tiles_per_seq, seg)
            x = _out_proj([gated], [p["w_out_c"][e]], x, mod, tiles_per_group, tm)
            cs.append(st[:, -(kc - 1):] if seg is None else st.reshape(b, t, d)[:, -(kc - 1):])
        x = _ffn(x, mod, tiles_per_group, tm, _ff_tile(p["w_gate"].shape[2]), p["norm_ffn_g"][l],
                 p["w_gate"][l], p["w_up"][l], p["w_down"][l], p["final_g"], final=(l == depth - 1))
    return (x.reshape(b, t, d), jnp.stack(ks), jnp.stack(vs), jnp.stack(fs), jnp.stack(bs), jnp.stack(cs))


def kernel(x_prompt, x_sample, cache_k, cache_v, cache_logf, state_convb, state_convc, c_prompt, c_sample,
           ada_w, ada_b, norm_mix_g, norm_ffn_g, w_in_ab, b_f, dw_b_w, dw_b_bias, ln_b_g, ln_b_b, w_out_ab,
           w_in_c, conv_c_w, w_out_c, w_gate, w_up, w_down, final_g):
    bp, tp, d = x_prompt.shape
    bs_, ts, _ = x_sample.shape
    depth = ada_w.shape[0]
    ha = b_f.shape[1]
    d_in = w_in_ab.shape[2]
    da = d_in - ha - 2 * w_out_ab.shape[1]
    db = w_out_ab.shape[1] - da
    dh = da // ha
    assert kc_ok(conv_c_w) and tp % SUBLANES == 0 and (bs_ * ts) % SUBLANES == 0

    bf = lambda a: a.astype(BF16)
    p = {
        "ha": ha, "dh": dh, "da": da, "db": db,
        "norm_mix_g": norm_mix_g.reshape(depth, 1, d), "norm_ffn_g": norm_ffn_g.reshape(depth, 1, d),
        "wqkv": bf(w_in_ab[:, :, 0:3 * da]),
        "wf": bf(w_in_ab[:, :, 3 * da:3 * da + ha].transpose(0, 2, 1)),
        "bf": b_f.reshape(-1, ha, 1),
        "wg_ab": bf(w_in_ab[:, :, 3 * da + ha:]),
        "dw_b_w": dw_b_w, "dw_b_bias": dw_b_bias.reshape(-1, 1, db),
        "ln_b_g": ln_b_g.reshape(-1, 1, db), "ln_b_b": ln_b_b.reshape(-1, 1, db),
        "wout_a": bf(w_out_ab[:, 0:da]), "wout_b": bf(w_out_ab[:, da:]),
        "w_in_c": bf(w_in_c), "conv_c_w": conv_c_w, "w_out_c": bf(w_out_c),
        "w_gate": bf(w_gate), "w_up": bf(w_up), "w_down": bf(w_down),
        "final_g": final_g.reshape(1, d),
    }

    nc = bp + bs_
    nc_pad = -(-nc // SUBLANES) * SUBLANES
    c_all = jnp.pad(jnp.concatenate([c_prompt, c_sample], axis=0), ((0, nc_pad - nc), (0, 0)))
    mod_all = _ada(c_all, ada_w, ada_b)
    mod_p = mod_all[:, 0:bp].reshape(depth, bp, 1, 6 * d)
    mod_s = jnp.repeat(mod_all[:, bp:nc], ts, axis=1).reshape(depth, 1, bs_ * ts, 6 * d)

    cache = {"k": cache_k.reshape(cache_k.shape[0], bs_, -1, da), "v": cache_v.reshape(cache_v.shape[0], bs_, -1, da),
             "logf": cache_logf, "convb": state_convb, "convc": state_convc}
    y_p, k_p, v_p, f_p, b_p, c_p = _trunk(x_prompt, mod_p, p, None)
    y_s, k_s, v_s, f_s, b_s, c_s = _trunk(x_sample, mod_s, p, cache)
    return (y_p, y_s, k_p, v_p, f_p, b_p, c_p, k_s, v_s, f_s, b_s, c_s)


def kc_ok(conv_c_w):
    return conv_c_w.shape[1] == 3
```

```python
import functools
import math

import jax
import jax.numpy as jnp
from jax import lax
from jax.experimental import pallas as pl
from jax.experimental.pallas import tpu as pltpu

RMS_EPS = 1e-6
LN_EPS = 1e-5
LOG2E = math.log2(math.e)

V7X_VMEM_BYTES = 64 * 1024 * 1024
VMEM_LIMIT = V7X_VMEM_BYTES - 8 * 1024 * 1024
LANES = 128
SUBLANES = 8
CONV_HALO = 32

F32 = jnp.float32
BF16 = jnp.bfloat16

SH1, SC1, G1, SH2, SC2, G2 = range(6)


def _cparams(n_axes):
    return pltpu.CompilerParams(dimension_semantics=("arbitrary",) * n_axes, vmem_limit_bytes=VMEM_LIMIT)


def _resident(shape):
    nd = len(shape)
    return pl.BlockSpec(shape, lambda *_: (0,) * nd, pipeline_mode=pl.Buffered(1))


def _mod_spec(mod, col, tiles_per_group, d):
    mr = mod.shape[1]
    return pl.BlockSpec((None, mr, d), lambda i, *_: (i // tiles_per_group, 0, col))


def _sigmoid(x):
    return 1.0 / (1.0 + jnp.exp(-x))


def _silu(x):
    return x * _sigmoid(x)


def _rms(x):
    return x * lax.rsqrt(jnp.mean(x * x, axis=-1, keepdims=True) + RMS_EPS)


def _modulate(x, g, shift, scale):
    return (_rms(x) * g) * (1.0 + scale) + shift


def _dot(a, b):
    return jnp.dot(a, b, preferred_element_type=F32)


def _dot_nt(a, b):
    return lax.dot_general(a, b, (((1,), (1,)), ((), ())), preferred_element_type=F32)


def _ada_kernel(c_ref, w_ref, b_ref, o_ref):
    a = _silu(c_ref[...]).astype(BF16)
    o_ref[...] = _dot(a, w_ref[...].astype(BF16)) + b_ref[...]


def _ada(c_all, ada_w, ada_b):
    depth, d, n = ada_w.shape
    r = c_all.shape[0]
    tn = 1024 if n % 1024 == 0 else n
    return pl.pallas_call(
        _ada_kernel,
        grid=(depth, n // tn),
        in_specs=[
            pl.BlockSpec((r, d), lambda l, j: (0, 0)),
            pl.BlockSpec((None, d, tn), lambda l, j: (l, 0, j)),
            pl.BlockSpec((None, 1, tn), lambda l, j: (l, 0, j)),
        ],
        out_specs=pl.BlockSpec((None, r, tn), lambda l, j: (l, 0, j)),
        out_shape=jax.ShapeDtypeStruct((depth, r, n), F32),
        compiler_params=_cparams(2),
        name="ada",
    )(c_all, ada_w, ada_b.reshape(depth, 1, n))


def _in_ab_kernel(x_ref, sh_ref, sc_ref, g_ref, wqkv_ref, wf_ref, bf_ref, wg_ref,
                  q_ref, k_ref, v_ref, u_ref, lf_ref, *, da, db, qscale):
    h = _modulate(x_ref[...], g_ref[...], sh_ref[...], sc_ref[...]).astype(BF16)
    q_ref[...] = (_dot(h, wqkv_ref[:, 0:da]) * qscale).astype(BF16)
    k_ref[...] = _dot(h, wqkv_ref[:, da:2 * da])
    v_ref[...] = _dot(h, wqkv_ref[:, 2 * da:3 * da])
    ga = _dot(h, wg_ref[:, 0:db])
    gb = _dot(h, wg_ref[:, db:2 * db])
    u_ref[...] = ga * _sigmoid(gb)
    fz = _dot_nt(wf_ref[...], h) + bf_ref[...]
    lf_ref[...] = jnp.minimum(fz, 0.0) - jnp.log1p(jnp.exp(-jnp.abs(fz)))


def _in_ab(x, mod, tiles_per_group, tm, g, wqkv, wf, bf, wg, dh):
    r, d = x.shape
    da = wqkv.shape[1] // 3
    db = wg.shape[1] // 2
    ha = wf.shape[0]
    row = lambda w: pl.BlockSpec((tm, w), lambda i: (i, 0))
    return pl.pallas_call(
        functools.partial(_in_ab_kernel, da=da, db=db, qscale=LOG2E * float(dh) ** -0.5),
        grid=(r // tm,),
        in_specs=[
            row(d),
            _mod_spec(mod, SH1, tiles_per_group, d),
            _mod_spec(mod, SC1, tiles_per_group, d),
            _resident((1, d)),
            _resident(wqkv.shape),
            _resident(wf.shape),
            _resident(bf.shape),
            _resident(wg.shape),
        ],
        out_specs=[row(da), row(da), row(da), row(db), pl.BlockSpec((ha, tm), lambda i: (0, i))],
        out_shape=[
            jax.ShapeDtypeStruct((r, da), BF16),
            jax.ShapeDtypeStruct((r, da), F32),
            jax.ShapeDtypeStruct((r, da), F32),
            jax.ShapeDtypeStruct((r, db), F32),
            jax.ShapeDtypeStruct((ha, r), F32),
        ],
        compiler_params=_cparams(1),
        name="in_ab",
    )(x, mod, mod, g, wqkv, wf, bf, wg)


def _cumsum_kernel(x_ref, o_ref):
    x = x_ref[...]
    t = x.shape[-1]
    lane = lax.broadcasted_iota(jnp.int32, x.shape, 1)
    s = 1
    while s < t:
        x = x + jnp.where(lane >= s, pltpu.roll(x, s, axis=1), 0.0)
        s *= 2
    o_ref[...] = x


def _cumsum(logf_t, nb, t):
    ha = logf_t.shape[0]
    return pl.pallas_call(
        _cumsum_kernel,
        grid=(nb,),
        in_specs=[pl.BlockSpec((ha, t), lambda b: (0, b))],
        out_specs=pl.BlockSpec((None, ha, t), lambda b: (b, 0, 0)),
        out_shape=jax.ShapeDtypeStruct((nb, ha, t), F32),
        compiler_params=_cparams(1),
        name="cumsum",
    )(logf_t)


def _attn_prompt_kernel(q_ref, k_ref, v_ref, l_ref, o_ref, kt_scr, va_scr, s_scr, p_scr, *, blk, nblk, dh):
    t = nblk * blk
    nl = blk // LANES
    kt_scr[...] = k_ref[...].T.astype(BF16)
    va_scr[:, 0:dh] = v_ref[...].astype(BF16)
    va_scr[:, dh:2 * dh] = jnp.ones((t, dh), BF16)
    row = lax.broadcasted_iota(jnp.int32, (blk, blk), 0)
    col = lax.broadcasted_iota(jnp.int32, (blk, blk), 1)

    for i in range(nblk):
        buf = i % 2
        q = q_ref[i * blk:(i + 1) * blk, :]
        mx = None
        for c in range(i + 1):
            cs = slice(c * blk, (c + 1) * blk)
            s = _dot(q, kt_scr[:, cs]) - l_ref[:, cs] * LOG2E
            if c == i:
                s = jnp.where(col <= row, s, -jnp.inf)
            s_scr[buf, :, cs] = s
            for l in range(nl):
                sl = s[:, l * LANES:(l + 1) * LANES]
                mx = sl if mx is None else jnp.maximum(mx, sl)
        mb = jnp.broadcast_to(jnp.max(mx, axis=-1, keepdims=True), (blk, LANES))
        for l in range((i + 1) * nl):
            ls = slice(l * LANES, (l + 1) * LANES)
            p_scr[buf, :, ls] = jnp.exp2(s_scr[buf, :, ls] - mb).astype(BF16)
        vis = (i + 1) * blk
        acc = _dot(p_scr[buf, :, 0:vis], va_scr[0:vis, :])
        o_ref[i * blk:(i + 1) * blk, :] = (acc[:, 0:dh] / acc[:, dh:2 * dh]).astype(o_ref.dtype)


def _attn_prompt(q, k, v, cum, b, t, ha, dh, blk):
    assert dh == LANES and blk % LANES == 0 and t % blk == 0
    nblk = t // blk
    head = pl.BlockSpec((t, dh), lambda bi, h: (bi, h))
    return pl.pallas_call(
        functools.partial(_attn_prompt_kernel, blk=blk, nblk=nblk, dh=dh),
        grid=(b, ha),
        in_specs=[head, head, head,
                  pl.BlockSpec((None, None, 1, t), lambda bi, h: (bi, h, 0, 0))],
        out_specs=head,
        out_shape=jax.ShapeDtypeStruct((b * t, ha * dh), BF16),
        scratch_shapes=[
            pltpu.VMEM((dh, t), BF16),
            pltpu.VMEM((t, 2 * dh), BF16),
            pltpu.VMEM((2, blk, t), F32),
            pltpu.VMEM((2, blk, t), BF16),
        ],
        compiler_params=_cparams(2),
        name="attn_prompt",
    )(q, k, v, cum.reshape(b, ha, 1, t))


def _attn_sample_kernel(q_ref, kn_ref, vn_ref, ck_ref, cv_ref, l_ref, o_ref, *, ha, dh, past, npad):
    s_new = q_ref.shape[0]
    row = lax.broadcasted_iota(jnp.int32, (s_new, npad), 0)
    col = lax.broadcasted_iota(jnp.int32, (s_new, npad), 1)
    for h in range(ha):
        hs = slice(h * dh, (h + 1) * dh)
        q = q_ref[:, hs]
        lrow = l_ref[h:h + 1, :] * LOG2E
        sc = _dot_nt(q, ck_ref[:, hs].astype(BF16)) - lrow[:, 0:past]
        sn = _dot_nt(q, kn_ref[:, hs].astype(BF16)) - lrow[:, past:past + npad]
        sn = jnp.where(col <= row, sn, -jnp.inf)
        m = jnp.maximum(jnp.max(sc, axis=-1, keepdims=True), jnp.max(sn, axis=-1, keepdims=True))
        pc = jnp.exp2(sc - m)
        pn = jnp.exp2(sn - m)
        denom = jnp.sum(pc, axis=-1, keepdims=True) + jnp.sum(pn, axis=-1, keepdims=True)
        acc = (_dot(pc.astype(BF16), cv_ref[:, hs].astype(BF16))
               + _dot(pn.astype(BF16), vn_ref[:, hs].astype(BF16)))
        o_ref[:, hs] = (acc / denom).astype(o_ref.dtype)


def _attn_sample(q, kn, vn, cache_k, cache_v, layer, cum, ha, dh):
    b, s, da = q.shape
    npad = kn.shape[1]
    past = cache_k.shape[2]
    tp = cum.shape[-1]
    per_b = lambda rows: pl.BlockSpec((None, rows, da), lambda bi: (bi, 0, 0))
    cache = pl.BlockSpec((None, None, past, da), lambda bi: (layer, bi, 0, 0))
    return pl.pallas_call(
        functools.partial(_attn_sample_kernel, ha=ha, dh=dh, past=past, npad=npad),
        grid=(b,),
        in_specs=[per_b(s), per_b(npad), per_b(npad), cache, cache,
                  pl.BlockSpec((None, ha, tp), lambda bi: (bi, 0, 0))],
        out_specs=per_b(s),
        out_shape=jax.ShapeDtypeStruct((b, s, da), BF16),
        compiler_params=_cparams(1),
        name="attn_sample",
    )(q, kn, vn, cache_k, cache_v, cum)


def _convb_kernel(u_ref, hist_ref, w_ref, b_ref, g_ref, beta_ref, z_ref, ext_scr, y_scr,
                  *, kb, tm, rc, tiles_per_seq):
    i = pl.program_id(1)
    c = u_ref.shape[-1]

    @pl.when(i == 0)
    def _():
        ext_scr[0:CONV_HALO, :] = hist_ref[...]

    ext_scr[CONV_HALO:CONV_HALO + tm, :] = u_ref[...]
    off = CONV_HALO - (kb - 1)
    for r0 in range(0, tm, rc):
        for c0 in range(0, c, LANES):
            acc = jnp.zeros((rc, LANES), F32)
            for k in range(kb):
                acc = acc + w_ref[k:k + 1, c0:c0 + LANES] * ext_scr[r0 + off + k:r0 + off + k + rc, c0:c0 + LANES]
            y_scr[r0:r0 + rc, c0:c0 + LANES] = acc
    y = y_scr[...] + b_ref[...]
    mu = jnp.mean(y, axis=-1, keepdims=True)
    yc = y - mu
    var = jnp.mean(yc * yc, axis=-1, keepdims=True)
    z = yc * lax.rsqrt(var + LN_EPS) * g_ref[...] + beta_ref[...]
    z_ref[...] = _silu(z).astype(z_ref.dtype)
    if tiles_per_seq > 1:
        ext_scr[0:CONV_HALO, :] = ext_scr[tm:tm + CONV_HALO, :]


def _convb(u, hist_pad, w, bias, g, beta, tm):
    b, t, c = u.shape
    kb = w.shape[0]
    assert kb - 1 <= CONV_HALO and t % tm == 0 and (tm >= CONV_HALO or t == tm)
    rc = min(tm, 64)
    return pl.pallas_call(
        functools.partial(_convb_kernel, kb=kb, tm=tm, rc=rc, tiles_per_seq=t // tm),
        grid=(b, t // tm),
        in_specs=[
            pl.BlockSpec((None, tm, c), lambda bi, i: (bi, i, 0)),
            pl.BlockSpec((None, CONV_HALO, c), lambda bi, i: (bi, 0, 0)),
            _resident(w.shape), _resident((1, c)), _resident((1, c)), _resident((1, c)),
        ],
        out_specs=pl.BlockSpec((None, tm, c), lambda bi, i: (bi, i, 0)),
        out_shape=jax.ShapeDtypeStruct((b, t, c), BF16),
        scratch_shapes=[pltpu.VMEM((CONV_HALO + tm, c), F32), pltpu.VMEM((tm, c), F32)],
        compiler_params=_cparams(2),
        name="convb",
    )(u, hist_pad, w, bias, g, beta)


def _in_c_kernel(x_ref, sh_ref, sc_ref, g_ref, w_ref, cw_ref, h0_ref, h1_ref, o_ref, st_ref, ext_scr,
                 *, dc, tm, tiles_per_seq, seg):
    h = _modulate(x_ref[...], g_ref[...], sh_ref[...], sc_ref[...]).astype(BF16)
    cx = _dot(h, w_ref[:, dc:2 * dc]) * _dot(h, w_ref[:, 2 * dc:3 * dc])
    if seg is None:
        i = pl.program_id(0) % tiles_per_seq

        @pl.when(i == 0)
        def _():
            ext_scr[SUBLANES - 2:SUBLANES - 1, :] = h0_ref[...]
            ext_scr[SUBLANES - 1:SUBLANES, :] = h1_ref[...]
    else:
        ext_scr[0:SUBLANES, :] = jnp.zeros((SUBLANES, dc), F32)

    ext_scr[SUBLANES:SUBLANES + tm, :] = cx
    m1 = ext_scr[SUBLANES - 1:SUBLANES - 1 + tm, :]
    m2 = ext_scr[SUBLANES - 2:SUBLANES - 2 + tm, :]
    if seg is not None:
        pos = lax.broadcasted_iota(jnp.int32, (tm, 1), 0) % seg
        m1 = jnp.where(pos == 0, h1_ref[...], m1)
        m2 = jnp.where(pos == 0, h0_ref[...], jnp.where(pos == 1, h1_ref[...], m2))
    uc = cw_ref[0:1, :] * m2 + cw_ref[1:2, :] * m1 + cw_ref[2:3, :] * cx
    o_ref[...] = (_dot(h, w_ref[:, 0:dc]) * uc).astype(o_ref.dtype)
    if seg is None:
        st_ref[...] = ext_scr[tm:tm + SUBLANES, :]
        if tiles_per_seq > 1:
            ext_scr[0:SUBLANES, :] = ext_scr[tm:tm + SUBLANES, :]
    else:
        st_ref[...] = cx


def _in_c(x, mod, tiles_per_group, tm, g, w, cw, h0, h1, tiles_per_seq, seg):
    r, d = x.shape
    dc = w.shape[1] // 3
    row = lambda width: pl.BlockSpec((tm, width), lambda i: (i, 0))
    if seg is None:
        nseq = r // (tm * tiles_per_seq)
        hspec = pl.BlockSpec((None, 1, dc), lambda i: (i // tiles_per_seq, 0, 0))
        st_spec = pl.BlockSpec((None, SUBLANES, dc), lambda i: (i // tiles_per_seq, 0, 0))
        st_shape = jax.ShapeDtypeStruct((nseq, SUBLANES, dc), F32)
    else:
        hspec = row(dc)
        st_spec = row(dc)
        st_shape = jax.ShapeDtypeStruct((r, dc), F32)
    return pl.pallas_call(
        functools.partial(_in_c_kernel, dc=dc, tm=tm, tiles_per_seq=tiles_per_seq, seg=seg),
        grid=(r // tm,),
        in_specs=[
            row(d),
            _mod_spec(mod, SH1, tiles_per_group, d),
            _mod_spec(mod, SC1, tiles_per_group, d),
            _resident((1, d)), _resident(w.shape), _resident(cw.shape), hspec, hspec,
        ],
        out_specs=[row(dc), st_spec],
        out_shape=[jax.ShapeDtypeStruct((r, dc), BF16), st_shape],
        scratch_shapes=[pltpu.VMEM((SUBLANES + tm, dc), F32)],
        compiler_params=_cparams(1),
        name="in_c",
    )(x, mod, mod, g, w, cw, h0, h1)


def _out_kernel(*refs, n_in):
    a_refs, w_refs = refs[:n_in], refs[n_in:2 * n_in]
    x_ref, gate_ref, sh_ref, sc_ref, g_ref, o_ref, h_ref = refs[2 * n_in:]
    y = _dot(a_refs[0][...], w_refs[0][...])
    for a_ref, w_ref in zip(a_refs[1:], w_refs[1:]):
        y = y + _dot(a_ref[...], w_ref[...])
    xn = x_ref[...] + gate_ref[...] * y
    o_ref[...] = xn
    h_ref[...] = _modulate(xn, g_ref[...], sh_ref[...], sc_ref[...]).astype(h_ref.dtype)


def _out_proj(acts, weights, x, mod, tiles_per_group, tm, g_ffn):
    r, d = x.shape
    row = lambda w: pl.BlockSpec((tm, w), lambda i: (i, 0))
    return pl.pallas_call(
        functools.partial(_out_kernel, n_in=len(acts)),
        grid=(r // tm,),
        in_specs=[row(a.shape[1]) for a in acts] + [_resident(w.shape) for w in weights]
        + [row(d), _mod_spec(mod, G1, tiles_per_group, d), _mod_spec(mod, SH2, tiles_per_group, d),
           _mod_spec(mod, SC2, tiles_per_group, d), _resident((1, d))],
        out_specs=[row(d), row(d)],
        out_shape=[jax.ShapeDtypeStruct((r, d), F32), jax.ShapeDtypeStruct((r, d), BF16)],
        compiler_params=_cparams(1),
        name="out_proj",
    )(*acts, *weights, x, mod, mod, mod, g_ffn)


def _ffn_kernel(h_ref, x_ref, gate_ref, wg_ref, wu_ref, wd_ref, fg_ref, o_ref, *, final):
    j = pl.program_id(1)

    @pl.when(j == 0)
    def _():
        o_ref[...] = jnp.zeros(o_ref.shape, F32)

    h = h_ref[...]
    act = (_silu(_dot(h, wg_ref[...])) * _dot(h, wu_ref[...])).astype(BF16)
    o_ref[...] += _dot(act, wd_ref[...])

    @pl.when(j == pl.num_programs(1) - 1)
    def _():
        xn = x_ref[...] + gate_ref[...] * o_ref[...]
        if final:
            xn = _rms(xn) * fg_ref[...]
        o_ref[...] = xn


def _ffn(h, x, mod, tiles_per_group, tm, tf, wg, wu, wd, final_g, final):
    r, d = x.shape
    f = wg.shape[1]
    row = pl.BlockSpec((tm, d), lambda i, j: (i, 0))
    return pl.pallas_call(
        functools.partial(_ffn_kernel, final=final),
        grid=(r // tm, f // tf),
        in_specs=[
            row, row,
            _mod_spec(mod, G2, tiles_per_group, d),
            pl.BlockSpec((d, tf), lambda i, j: (0, j)),
            pl.BlockSpec((d, tf), lambda i, j: (0, j)),
            pl.BlockSpec((tf, d), lambda i, j: (j, 0)),
            _resident((1, d)),
        ],
        out_specs=row,
        out_shape=jax.ShapeDtypeStruct((r, d), F32),
        compiler_params=_cparams(2),
        name="ffn",
    )(h, x, mod, wg, wu, wd, final_g)


def _row_tile(t, cap):
    tm = min(t, cap)
    while t % tm:
        tm -= SUBLANES
    return tm


def _ff_tile(f, cap=512):
    tf = min(f, cap)
    while f % tf or tf % LANES:
        tf -= LANES
    return tf


def _trunk(x3, mods, p, cache):
    b, t, d = x3.shape
    depth = mods.shape[0]
    ha, dh, da, db = p["ha"], p["dh"], p["da"], p["db"]
    kb, kc = p["dw_b_w"].shape[1], p["conv_c_w"].shape[1]
    r = b * t
    x = x3.reshape(r, d)
    if cache is None:
        tm = _row_tile(t, 512)
        tiles_per_group = t // tm
        tiles_per_seq, seg = t // tm, None
    else:
        tm = r
        tiles_per_group = 1
        tiles_per_seq, seg = 1, t
    ks, vs, fs, bs, cs = [], [], [], [], []
    for l in range(depth):
        mod = mods[l]
        e = l // 2
        g_ffn = p["norm_ffn_g"][l]
        if l % 2 == 0:
            q, k, v, u, lf_t = _in_ab(x, mod, tiles_per_group, tm, p["norm_mix_g"][l],
                                      p["wqkv"][e], p["wf"][e], p["bf"][e], p["wg_ab"][e], dh)
            u3 = u.reshape(b, t, db)
            if cache is None:
                cum = _cumsum(lf_t, b, t)
                att = _attn_prompt(q, k, v, cum, b, t, ha, dh, _row_tile(t, 512))
                hist = jnp.zeros((b, kb - 1, db), F32)
            else:
                past = cache["k"].shape[2]
                npad = LANES
                lf_new = lf_t.reshape(ha, b, t).transpose(1, 0, 2)
                lf_all = jnp.concatenate([cache["logf"][e].transpose(0, 2, 1), lf_new,
                                          jnp.zeros((b, ha, npad - t), F32)], axis=-1)
                cum = _cumsum(lf_all.transpose(1, 0, 2).reshape(ha, b * (past + npad)), b, past + npad)
                pad = lambda a: jnp.pad(a.reshape(b, t, da), ((0, 0), (0, npad - t), (0, 0)))
                att = _attn_sample(q.reshape(b, t, da), pad(k), pad(v), cache["k"], cache["v"], e,
                                   cum, ha, dh).reshape(r, da)
                hist = cache["convb"][e]
            hist_pad = jnp.pad(hist, ((0, 0), (CONV_HALO - (kb - 1), 0), (0, 0)))
            z = _convb(u3, hist_pad, p["dw_b_w"][e], p["dw_b_bias"][e], p["ln_b_g"][e], p["ln_b_b"][e],
                       _row_tile(t, 256)).reshape(r, db)
            x, h = _out_proj([att, z], [p["wout_a"][e], p["wout_b"][e]], x, mod, tiles_per_group, tm, g_ffn)
            ks.append(k.reshape(b, t, ha, dh))
            vs.append(v.reshape(b, t, ha, dh))
            fs.append(lf_t.reshape(ha, b, t).transpose(1, 2, 0))
            bs.append(jnp.concatenate([hist, u3], axis=1)[:, -(kb - 1):])
        else:
            hist = jnp.zeros((b, kc - 1, d), F32) if cache is None else cache["convc"][e]
            if seg is None:
                h0, h1 = hist[:, 0:1], hist[:, 1:2]
            else:
                h0, h1 = jnp.repeat(hist[:, 0], t, axis=0), jnp.repeat(hist[:, 1], t, axis=0)
            gated, st = _in_c(x, mod, tiles_per_group, tm, p["norm_mix_g"][l], p["w_in_c"][e],
                              p["conv_c_w"][e], h0, h1, tiles_per_seq, seg)
            x, h = _out_proj([gated], [p["w_out_c"][e]], x, mod, tiles_per_group, tm, g_ffn)
            cs.append(st[:, -(kc - 1):] if seg is None else st.reshape(b, t, d)[:, -(kc - 1):])
        x = _ffn(h, x, mod, tiles_per_group, tm, _ff_tile(p["w_gate"].shape[2]),
                 p["w_gate"][l], p["w_up"][l], p["w_down"][l], p["final_g"], final=(l == depth - 1))
    return (x.reshape(b, t, d), jnp.stack(ks), jnp.stack(vs), jnp.stack(fs), jnp.stack(bs), jnp.stack(cs))


def kernel(x_prompt, x_sample, cache_k, cache_v, cache_logf, state_convb, state_convc, c_prompt, c_sample,
           ada_w, ada_b, norm_mix_g, norm_ffn_g, w_in_ab, b_f, dw_b_w, dw_b_bias, ln_b_g, ln_b_b, w_out_ab,
           w_in_c, conv_c_w, w_out_c, w_gate, w_up, w_down, final_g):
    bp, tp, d = x_prompt.shape
    bs_, ts, _ = x_sample.shape
    depth = ada_w.shape[0]
    ha = b_f.shape[1]
    d_in = w_in_ab.shape[2]
    da = d_in - ha - 2 * w_out_ab.shape[1]
    db = w_out_ab.shape[1] - da
    dh = da // ha
    assert conv_c_w.shape[1] == 3 and tp % SUBLANES == 0 and (bs_ * ts) % SUBLANES == 0

    bf = lambda a: a.astype(BF16)
    p = {
        "ha": ha, "dh": dh, "da": da, "db": db,
        "norm_mix_g": norm_mix_g.reshape(depth, 1, d), "norm_ffn_g": norm_ffn_g.reshape(depth, 1, d),
        "wqkv": bf(w_in_ab[:, :, 0:3 * da]),
        "wf": bf(w_in_ab[:, :, 3 * da:3 * da + ha].transpose(0, 2, 1)),
        "bf": b_f.reshape(-1, ha, 1),
        "wg_ab": bf(w_in_ab[:, :, 3 * da + ha:]),
        "dw_b_w": dw_b_w, "dw_b_bias": dw_b_bias.reshape(-1, 1, db),
        "ln_b_g": ln_b_g.reshape(-1, 1, db), "ln_b_b": ln_b_b.reshape(-1, 1, db),
        "wout_a": bf(w_out_ab[:, 0:da]), "wout_b": bf(w_out_ab[:, da:]),
        "w_in_c": bf(w_in_c), "conv_c_w": conv_c_w, "w_out_c": bf(w_out_c),
        "w_gate": bf(w_gate), "w_up": bf(w_up), "w_down": bf(w_down),
        "final_g": final_g.reshape(1, d),
    }

    nc = bp + bs_
    nc_pad = -(-nc // SUBLANES) * SUBLANES
    c_all = jnp.pad(jnp.concatenate([c_prompt, c_sample], axis=0), ((0, nc_pad - nc), (0, 0)))
    mod_all = _ada(c_all, ada_w, ada_b)
    mod_p = mod_all[:, 0:bp].reshape(depth, bp, 1, 6 * d)
    mod_s = jnp.repeat(mod_all[:, bp:nc], ts, axis=1).reshape(depth, 1, bs_ * ts, 6 * d)

    cache = {"k": cache_k.reshape(cache_k.shape[0], bs_, -1, da), "v": cache_v.reshape(cache_v.shape[0], bs_, -1, da),
             "logf": cache_logf, "convb": state_convb, "convc": state_convc}
    y_p, k_p, v_p, f_p, b_p, c_p = _trunk(x_prompt, mod_p, p, None)
    y_s, k_s, v_s, f_s, b_s, c_s = _trunk(x_sample, mod_s, p, cache)
    return (y_p, y_s, k_p, v_p, f_p, b_p, c_p, k_s, v_s, f_s, b_s, c_s)
```

```python
import functools
import math

import jax
import jax.numpy as jnp
from jax import lax
from jax.experimental import pallas as pl
from jax.experimental.pallas import tpu as pltpu

RMS_EPS = 1e-6
LN_EPS = 1e-5
LOG2E = math.log2(math.e)

V7X_VMEM_BYTES = 64 * 1024 * 1024
VMEM_LIMIT = V7X_VMEM_BYTES - 8 * 1024 * 1024
LANES = 128
SUBLANES = 8
CONV_HALO = 32

F32 = jnp.float32
BF16 = jnp.bfloat16

SH1, SC1, G1, SH2, SC2, G2 = range(6)


def _cparams(n_axes):
    return pltpu.CompilerParams(dimension_semantics=("arbitrary",) * n_axes, vmem_limit_bytes=VMEM_LIMIT)


def _resident(shape):
    nd = len(shape)
    return pl.BlockSpec(shape, lambda *_: (0,) * nd, pipeline_mode=pl.Buffered(1))


def _resident_layer(stack, layer):
    nd = stack.ndim - 1
    return pl.BlockSpec((None,) + stack.shape[1:], lambda *_: (layer,) + (0,) * nd, pipeline_mode=pl.Buffered(1))


def _mod_spec(mod, col, tiles_per_group, d):
    mr = mod.shape[1]
    return pl.BlockSpec((None, mr, d), lambda i, *_: (i // tiles_per_group, 0, col))


def _sigmoid(x):
    return 1.0 / (1.0 + jnp.exp(-x))


def _silu(x):
    return x * _sigmoid(x)


def _rms(x):
    return x * lax.rsqrt(jnp.mean(x * x, axis=-1, keepdims=True) + RMS_EPS)


def _modulate(x, g, shift, scale):
    return (_rms(x) * g) * (1.0 + scale) + shift


def _dot(a, b):
    return jnp.dot(a, b, preferred_element_type=F32)


def _dot_nt(a, b):
    return lax.dot_general(a, b, (((1,), (1,)), ((), ())), preferred_element_type=F32)


def _ada_kernel(c_ref, w_ref, b_ref, o_ref):
    a = _silu(c_ref[...]).astype(BF16)
    o_ref[...] = _dot(a, w_ref[...].astype(BF16)) + b_ref[...]


def _ada(c_all, ada_w, ada_b):
    depth, d, n = ada_w.shape
    r = c_all.shape[0]
    tn = 1024 if n % 1024 == 0 else n
    return pl.pallas_call(
        _ada_kernel,
        grid=(depth, n // tn),
        in_specs=[
            pl.BlockSpec((r, d), lambda l, j: (0, 0)),
            pl.BlockSpec((None, d, tn), lambda l, j: (l, 0, j)),
            pl.BlockSpec((None, 1, tn), lambda l, j: (l, 0, j)),
        ],
        out_specs=pl.BlockSpec((None, r, tn), lambda l, j: (l, 0, j)),
        out_shape=jax.ShapeDtypeStruct((depth, r, n), F32),
        compiler_params=_cparams(2),
        name="ada",
    )(c_all, ada_w, ada_b.reshape(depth, 1, n))


def _in_ab_kernel(*refs, da, db, qscale, n_prev):
    x_ref, sh_ref, sc_ref, g_ref, wqkv_ref, wf_ref, bf_ref, wg_ref = refs[:8]
    q_ref, k_ref, v_ref, u_ref, lf_ref = refs[8 + n_prev:]
    h = _modulate(x_ref[...], g_ref[...], sh_ref[...], sc_ref[...]).astype(BF16)
    q_ref[...] = (_dot(h, wqkv_ref[:, 0:da]) * qscale).astype(BF16)
    k_ref[...] = _dot(h, wqkv_ref[:, da:2 * da])
    v_ref[...] = _dot(h, wqkv_ref[:, 2 * da:3 * da])
    ga = _dot(h, wg_ref[:, 0:db])
    gb = _dot(h, wg_ref[:, db:2 * db])
    u_ref[...] = ga * _sigmoid(gb)
    fz = _dot_nt(wf_ref[...], h) + bf_ref[...]
    lf_ref[...] = jnp.minimum(fz, 0.0) - jnp.log1p(jnp.exp(-jnp.abs(fz)))


def _in_ab(x, mod, tiles_per_group, tm, g, wqkv, wf, bf, wg, e, n_even, kv_prev, dh):
    r, d = x.shape
    da = wqkv.shape[2] // 3
    db = wg.shape[2] // 2
    ha = wf.shape[1]
    row = lambda w: pl.BlockSpec((tm, w), lambda i: (i, 0))
    kv_spec = pl.BlockSpec((None, tm, da), lambda i: (e, i, 0))
    kv_shape = jax.ShapeDtypeStruct((n_even, r, da), F32)
    n_prev = 0 if kv_prev is None else 2
    outs = pl.pallas_call(
        functools.partial(_in_ab_kernel, da=da, db=db, qscale=LOG2E * float(dh) ** -0.5, n_prev=n_prev),
        grid=(r // tm,),
        in_specs=[
            row(d),
            _mod_spec(mod, SH1, tiles_per_group, d),
            _mod_spec(mod, SC1, tiles_per_group, d),
            _resident((1, d)),
            _resident_layer(wqkv, e), _resident_layer(wf, e), _resident_layer(bf, e), _resident_layer(wg, e),
        ] + [pl.BlockSpec(memory_space=pl.ANY)] * n_prev,
        out_specs=[row(da), kv_spec, kv_spec, row(db), pl.BlockSpec((ha, tm), lambda i: (0, i))],
        out_shape=[
            jax.ShapeDtypeStruct((r, da), BF16),
            kv_shape, kv_shape,
            jax.ShapeDtypeStruct((r, db), F32),
            jax.ShapeDtypeStruct((ha, r), F32),
        ],
        input_output_aliases={8: 1, 9: 2} if n_prev else {},
        compiler_params=_cparams(1),
        name="in_ab",
    )(x, mod, mod, g, wqkv, wf, bf, wg, *(kv_prev or ()))
    return outs


def _cumsum_kernel(x_ref, o_ref):
    x = x_ref[...]
    t = x.shape[-1]
    lane = lax.broadcasted_iota(jnp.int32, x.shape, 1)
    s = 1
    while s < t:
        x = x + jnp.where(lane >= s, pltpu.roll(x, s, axis=1), 0.0)
        s *= 2
    o_ref[...] = x


def _cumsum(logf_t, nb, t):
    ha = logf_t.shape[0]
    return pl.pallas_call(
        _cumsum_kernel,
        grid=(nb,),
        in_specs=[pl.BlockSpec((ha, t), lambda b: (0, b))],
        out_specs=pl.BlockSpec((None, ha, t), lambda b: (b, 0, 0)),
        out_shape=jax.ShapeDtypeStruct((nb, ha, t), F32),
        compiler_params=_cparams(1),
        name="cumsum",
    )(logf_t)


def _attn_prompt_kernel(q_ref, k_ref, v_ref, l_ref, o_ref, kt_scr, va_scr, s_scr, p_scr, *, blk, nblk, dh):
    t = nblk * blk
    nl = blk // LANES
    kt_scr[...] = k_ref[...].T.astype(BF16)
    va_scr[:, 0:dh] = v_ref[...].astype(BF16)
    va_scr[:, dh:2 * dh] = jnp.ones((t, dh), BF16)
    row = lax.broadcasted_iota(jnp.int32, (blk, blk), 0)
    col = lax.broadcasted_iota(jnp.int32, (blk, blk), 1)

    for i in range(nblk):
        buf = i % 2
        q = q_ref[i * blk:(i + 1) * blk, :]
        mx = None
        for c in range(i + 1):
            cs = slice(c * blk, (c + 1) * blk)
            s = _dot(q, kt_scr[:, cs]) - l_ref[:, cs] * LOG2E
            if c == i:
                s = jnp.where(col <= row, s, -jnp.inf)
            s_scr[buf, :, cs] = s
            for l in range(nl):
                sl = s[:, l * LANES:(l + 1) * LANES]
                mx = sl if mx is None else jnp.maximum(mx, sl)
        mb = jnp.broadcast_to(jnp.max(mx, axis=-1, keepdims=True), (blk, LANES))
        for l in range((i + 1) * nl):
            ls = slice(l * LANES, (l + 1) * LANES)
            p_scr[buf, :, ls] = jnp.exp2(s_scr[buf, :, ls] - mb).astype(BF16)
        vis = (i + 1) * blk
        acc = _dot(p_scr[buf, :, 0:vis], va_scr[0:vis, :])
        o_ref[i * blk:(i + 1) * blk, :] = (acc[:, 0:dh] / acc[:, dh:2 * dh]).astype(o_ref.dtype)


def _attn_prompt(q, k, v, e, cum, b, t, ha, dh, blk):
    assert dh == LANES and blk % LANES == 0 and t % blk == 0
    nblk = t // blk
    head = pl.BlockSpec((t, dh), lambda bi, h: (bi, h))
    kv_head = pl.BlockSpec((None, t, dh), lambda bi, h: (e, bi, h))
    return pl.pallas_call(
        functools.partial(_attn_prompt_kernel, blk=blk, nblk=nblk, dh=dh),
        grid=(b, ha),
        in_specs=[head, kv_head, kv_head,
                  pl.BlockSpec((None, None, 1, t), lambda bi, h: (bi, h, 0, 0))],
        out_specs=head,
        out_shape=jax.ShapeDtypeStruct((b * t, ha * dh), BF16),
        scratch_shapes=[
            pltpu.VMEM((dh, t), BF16),
            pltpu.VMEM((t, 2 * dh), BF16),
            pltpu.VMEM((2, blk, t), F32),
            pltpu.VMEM((2, blk, t), BF16),
        ],
        compiler_params=_cparams(2),
        name="attn_prompt",
    )(q, k, v, cum.reshape(b, ha, 1, t))


def _attn_sample_kernel(q_ref, kn_ref, vn_ref, ck_ref, cv_ref, l_ref, o_ref, *, ha, dh, past, npad):
    s_new = q_ref.shape[0]
    row = lax.broadcasted_iota(jnp.int32, (s_new, npad), 0)
    col = lax.broadcasted_iota(jnp.int32, (s_new, npad), 1)
    for h in range(ha):
        hs = slice(h * dh, (h + 1) * dh)
        q = q_ref[:, hs]
        lrow = l_ref[h:h + 1, :] * LOG2E
        sc = _dot_nt(q, ck_ref[:, h, :].astype(BF16)) - lrow[:, 0:past]
        sn = _dot_nt(q, kn_ref[:, hs].astype(BF16)) - lrow[:, past:past + npad]
        sn = jnp.where(col <= row, sn, -jnp.inf)
        m = jnp.maximum(jnp.max(sc, axis=-1, keepdims=True), jnp.max(sn, axis=-1, keepdims=True))
        pc = jnp.exp2(sc - m)
        pn = jnp.exp2(sn - m)
        denom = jnp.sum(pc, axis=-1, keepdims=True) + jnp.sum(pn, axis=-1, keepdims=True)
        acc = (_dot(pc.astype(BF16), cv_ref[:, h, :].astype(BF16))
               + _dot(pn.astype(BF16), vn_ref[:, hs].astype(BF16)))
        o_ref[:, hs] = (acc / denom).astype(o_ref.dtype)


def _attn_sample(q, kn, vn, cache_k, cache_v, layer, cum):
    b, s, da = q.shape
    npad = kn.shape[1]
    _, _, past, ha, dh = cache_k.shape
    tp = cum.shape[-1]
    per_b = lambda rows: pl.BlockSpec((None, rows, da), lambda bi: (bi, 0, 0))
    cache = pl.BlockSpec((None, None, past, ha, dh), lambda bi: (layer, bi, 0, 0, 0))
    return pl.pallas_call(
        functools.partial(_attn_sample_kernel, ha=ha, dh=dh, past=past, npad=npad),
        grid=(b,),
        in_specs=[per_b(s), per_b(npad), per_b(npad), cache, cache,
                  pl.BlockSpec((None, ha, tp), lambda bi: (bi, 0, 0))],
        out_specs=per_b(s),
        out_shape=jax.ShapeDtypeStruct((b, s, da), BF16),
        compiler_params=_cparams(1),
        name="attn_sample",
    )(q, kn, vn, cache_k, cache_v, cum)


def _convb_kernel(u_ref, hist_ref, w_ref, b_ref, g_ref, beta_ref, z_ref, ext_scr, zs_scr, y_scr,
                  *, kb, tm, rc, tiles_per_seq):
    i = pl.program_id(1)
    c = u_ref.shape[-1]

    @pl.when(i == 0)
    def _():
        ext_scr[0:CONV_HALO, :] = hist_ref[...]

    ext_scr[CONV_HALO:CONV_HALO + tm, :] = u_ref[...]
    off = CONV_HALO - (kb - 1)
    for r0 in range(0, tm, rc):
        for c0 in range(0, c, LANES):
            cl = slice(c0, c0 + LANES)
            y = None
            for s in range(SUBLANES):
                rows = rc if s == 0 else rc + SUBLANES
                zs = None
                for j in range(off, off + kb):
                    if j % SUBLANES != s:
                        continue
                    term = w_ref[j - off:j - off + 1, cl] * ext_scr[r0 + j - s:r0 + j - s + rows, cl]
                    zs = term if zs is None else zs + term
                if zs is None:
                    continue
                if s == 0:
                    shifted = zs
                else:
                    zs_scr[s, :, :] = zs
                    shifted = zs_scr[s, s:s + rc, :]
                y = shifted if y is None else y + shifted
            y_scr[r0:r0 + rc, cl] = y
    y = y_scr[...] + b_ref[...]
    mu = jnp.mean(y, axis=-1, keepdims=True)
    yc = y - mu
    var = jnp.mean(yc * yc, axis=-1, keepdims=True)
    z = yc * lax.rsqrt(var + LN_EPS) * g_ref[...] + beta_ref[...]
    z_ref[...] = _silu(z).astype(z_ref.dtype)
    if tiles_per_seq > 1:
        ext_scr[0:CONV_HALO, :] = ext_scr[tm:tm + CONV_HALO, :]


def _convb(u, hist_pad, w, bias, g, beta, tm):
    b, t, c = u.shape
    kb = w.shape[0]
    assert kb - 1 <= CONV_HALO and t % tm == 0 and (tm >= CONV_HALO or t == tm)
    rc = min(tm, 64)
    return pl.pallas_call(
        functools.partial(_convb_kernel, kb=kb, tm=tm, rc=rc, tiles_per_seq=t // tm),
        grid=(b, t // tm),
        in_specs=[
            pl.BlockSpec((None, tm, c), lambda bi, i: (bi, i, 0)),
            pl.BlockSpec((None, CONV_HALO, c), lambda bi, i: (bi, 0, 0)),
            _resident(w.shape), _resident((1, c)), _resident((1, c)), _resident((1, c)),
        ],
        out_specs=pl.BlockSpec((None, tm, c), lambda bi, i: (bi, i, 0)),
        out_shape=jax.ShapeDtypeStruct((b, t, c), BF16),
        scratch_shapes=[pltpu.VMEM((CONV_HALO + tm, c), F32),
                        pltpu.VMEM((SUBLANES, rc + SUBLANES, LANES), F32),
                        pltpu.VMEM((tm, c), F32)],
        compiler_params=_cparams(2),
        name="convb",
    )(u, hist_pad, w, bias, g, beta)


def _in_c_kernel(x_ref, sh_ref, sc_ref, g_ref, w_ref, cw_ref, h0_ref, h1_ref, o_ref, st_ref, ext_scr,
                 *, dc, tm, tiles_per_seq, seg):
    h = _modulate(x_ref[...], g_ref[...], sh_ref[...], sc_ref[...]).astype(BF16)
    cx = _dot(h, w_ref[:, dc:2 * dc]) * _dot(h, w_ref[:, 2 * dc:3 * dc])
    if seg is None:
        i = pl.program_id(0) % tiles_per_seq

        @pl.when(i == 0)
        def _():
            ext_scr[SUBLANES - 2:SUBLANES - 1, :] = h0_ref[...]
            ext_scr[SUBLANES - 1:SUBLANES, :] = h1_ref[...]
    else:
        ext_scr[0:SUBLANES, :] = jnp.zeros((SUBLANES, dc), F32)

    ext_scr[SUBLANES:SUBLANES + tm, :] = cx
    m1 = ext_scr[SUBLANES - 1:SUBLANES - 1 + tm, :]
    m2 = ext_scr[SUBLANES - 2:SUBLANES - 2 + tm, :]
    if seg is not None:
        pos = lax.broadcasted_iota(jnp.int32, (tm, 1), 0) % seg
        m1 = jnp.where(pos == 0, h1_ref[...], m1)
        m2 = jnp.where(pos == 0, h0_ref[...], jnp.where(pos == 1, h1_ref[...], m2))
    uc = cw_ref[0:1, :] * m2 + cw_ref[1:2, :] * m1 + cw_ref[2:3, :] * cx
    o_ref[...] = (_dot(h, w_ref[:, 0:dc]) * uc).astype(o_ref.dtype)
    if seg is None:
        st_ref[...] = ext_scr[tm:tm + SUBLANES, :]
        if tiles_per_seq > 1:
            ext_scr[0:SUBLANES, :] = ext_scr[tm:tm + SUBLANES, :]
    else:
        st_ref[...] = cx


def _in_c(x, mod, tiles_per_group, tm, g, w, e, cw, h0, h1, tiles_per_seq, seg):
    r, d = x.shape
    dc = w.shape[2] // 3
    row = lambda width: pl.BlockSpec((tm, width), lambda i: (i, 0))
    if seg is None:
        nseq = r // (tm * tiles_per_seq)
        hspec = pl.BlockSpec((None, 1, dc), lambda i: (i // tiles_per_seq, 0, 0))
        st_spec = pl.BlockSpec((None, SUBLANES, dc), lambda i: (i // tiles_per_seq, 0, 0))
        st_shape = jax.ShapeDtypeStruct((nseq, SUBLANES, dc), F32)
    else:
        hspec = row(dc)
        st_spec = row(dc)
        st_shape = jax.ShapeDtypeStruct((r, dc), F32)
    return pl.pallas_call(
        functools.partial(_in_c_kernel, dc=dc, tm=tm, tiles_per_seq=tiles_per_seq, seg=seg),
        grid=(r // tm,),
        in_specs=[
            row(d),
            _mod_spec(mod, SH1, tiles_per_group, d),
            _mod_spec(mod, SC1, tiles_per_group, d),
            _resident((1, d)), _resident_layer(w, e), _resident(cw.shape), hspec, hspec,
        ],
        out_specs=[row(dc), st_spec],
        out_shape=[jax.ShapeDtypeStruct((r, dc), BF16), st_shape],
        scratch_shapes=[pltpu.VMEM((SUBLANES + tm, dc), F32)],
        compiler_params=_cparams(1),
        name="in_c",
    )(x, mod, mod, g, w, cw, h0, h1)


def _out_kernel(*refs, n_in):
    a_refs, w_refs = refs[:n_in], refs[n_in:2 * n_in]
    x_ref, gate_ref, sh_ref, sc_ref, g_ref, o_ref, h_ref = refs[2 * n_in:]
    y = _dot(a_refs[0][...], w_refs[0][...])
    for a_ref, w_ref in zip(a_refs[1:], w_refs[1:]):
        y = y + _dot(a_ref[...], w_ref[...])
    xn = x_ref[...] + gate_ref[...] * y
    o_ref[...] = xn
    h_ref[...] = _modulate(xn, g_ref[...], sh_ref[...], sc_ref[...]).astype(h_ref.dtype)


def _out_proj(acts, weights, e, x, mod, tiles_per_group, tm, g_ffn):
    r, d = x.shape
    row = lambda w: pl.BlockSpec((tm, w), lambda i: (i, 0))
    return pl.pallas_call(
        functools.partial(_out_kernel, n_in=len(acts)),
        grid=(r // tm,),
        in_specs=[row(a.shape[1]) for a in acts] + [_resident_layer(w, e) for w in weights]
        + [row(d), _mod_spec(mod, G1, tiles_per_group, d), _mod_spec(mod, SH2, tiles_per_group, d),
           _mod_spec(mod, SC2, tiles_per_group, d), _resident((1, d))],
        out_specs=[row(d), row(d)],
        out_shape=[jax.ShapeDtypeStruct((r, d), F32), jax.ShapeDtypeStruct((r, d), BF16)],
        compiler_params=_cparams(1),
        name="out_proj",
    )(*acts, *weights, x, mod, mod, mod, g_ffn)


def _ffn_kernel(h_ref, x_ref, gate_ref, wg_ref, wu_ref, wd_ref, fg_ref, o_ref, *, final):
    j = pl.program_id(1)

    @pl.when(j == 0)
    def _():
        o_ref[...] = jnp.zeros(o_ref.shape, F32)

    h = h_ref[...]
    act = (_silu(_dot(h, wg_ref[...])) * _dot(h, wu_ref[...])).astype(BF16)
    o_ref[...] += _dot(act, wd_ref[...])

    @pl.when(j == pl.num_programs(1) - 1)
    def _():
        xn = x_ref[...] + gate_ref[...] * o_ref[...]
        if final:
            xn = _rms(xn) * fg_ref[...]
        o_ref[...] = xn


def _ffn(h, x, mod, tiles_per_group, tm, tf, wg, wu, wd, layer, final_g, final):
    r, d = x.shape
    f = wg.shape[2]
    row = pl.BlockSpec((tm, d), lambda i, j: (i, 0))
    return pl.pallas_call(
        functools.partial(_ffn_kernel, final=final),
        grid=(r // tm, f // tf),
        in_specs=[
            row, row,
            _mod_spec(mod, G2, tiles_per_group, d),
            pl.BlockSpec((None, d, tf), lambda i, j: (layer, 0, j)),
            pl.BlockSpec((None, d, tf), lambda i, j: (layer, 0, j)),
            pl.BlockSpec((None, tf, d), lambda i, j: (layer, j, 0)),
            _resident((1, d)),
        ],
        out_specs=row,
        out_shape=jax.ShapeDtypeStruct((r, d), F32),
        compiler_params=_cparams(2),
        name="ffn",
    )(h, x, mod, wg, wu, wd, final_g)


def _row_tile(t, cap):
    tm = min(t, cap)
    while t % tm:
        tm -= SUBLANES
    return tm


def _ff_tile(f, cap=512):
    tf = min(f, cap)
    while f % tf or tf % LANES:
        tf -= LANES
    return tf


def _trunk(x3, mods, p, cache):
    b, t, d = x3.shape
    depth = mods.shape[0]
    n_even = (depth + 1) // 2
    ha, dh, da, db = p["ha"], p["dh"], p["da"], p["db"]
    kb, kc = p["dw_b_w"].shape[1], p["conv_c_w"].shape[1]
    r = b * t
    x = x3.reshape(r, d)
    if cache is None:
        tm = _row_tile(t, 512)
        tiles_per_group = t // tm
        tiles_per_seq, seg = t // tm, None
    else:
        tm = r
        tiles_per_group = 1
        tiles_per_seq, seg = 1, t
    kv = None
    fs, bs, cs = [], [], []
    for l in range(depth):
        mod = mods[l]
        e = l // 2
        g_ffn = p["norm_ffn_g"][l]
        if l % 2 == 0:
            q, k_all, v_all, u, lf_t = _in_ab(x, mod, tiles_per_group, tm, p["norm_mix_g"][l], p["wqkv"],
                                              p["wf"], p["bf"], p["wg_ab"], e, n_even, kv, dh)
            kv = (k_all, v_all)
            u3 = u.reshape(b, t, db)
            if cache is None:
                cum = _cumsum(lf_t, b, t)
                att = _attn_prompt(q, k_all, v_all, e, cum, b, t, ha, dh, _row_tile(t, 512))
                hist = jnp.zeros((b, kb - 1, db), F32)
            else:
                past = cache["k"].shape[2]
                npad = LANES
                lf_new = lf_t.reshape(ha, b, t).transpose(1, 0, 2)
                lf_all = jnp.concatenate([cache["logf"][e].transpose(0, 2, 1), lf_new,
                                          jnp.zeros((b, ha, npad - t), F32)], axis=-1)
                cum = _cumsum(lf_all.transpose(1, 0, 2).reshape(ha, b * (past + npad)), b, past + npad)
                pad = lambda a: jnp.pad(a[e].reshape(b, t, da), ((0, 0), (0, npad - t), (0, 0)))
                att = _attn_sample(q.reshape(b, t, da), pad(k_all), pad(v_all), cache["k"], cache["v"], e,
                                   cum).reshape(r, da)
                hist = cache["convb"][e]
            hist_pad = jnp.pad(hist, ((0, 0), (CONV_HALO - (kb - 1), 0), (0, 0)))
            z = _convb(u3, hist_pad, p["dw_b_w"][e], p["dw_b_bias"][e], p["ln_b_g"][e], p["ln_b_b"][e],
                       _row_tile(t, 256)).reshape(r, db)
            x, h = _out_proj([att, z], [p["wout_a"], p["wout_b"]], e, x, mod, tiles_per_group, tm, g_ffn)
            fs.append(lf_t.reshape(ha, b, t).transpose(1, 2, 0))
            keep = kb - 1
            bs.append(u3[:, t - keep:] if t >= keep else jnp.concatenate([hist[:, t:], u3], axis=1))
        else:
            hist = jnp.zeros((b, kc - 1, d), F32) if cache is None else cache["convc"][e]
            if seg is None:
                h0, h1 = hist[:, 0:1], hist[:, 1:2]
            else:
                h0, h1 = jnp.repeat(hist[:, 0], t, axis=0), jnp.repeat(hist[:, 1], t, axis=0)
            gated, st = _in_c(x, mod, tiles_per_group, tm, p["norm_mix_g"][l], p["w_in_c"], e,
                              p["conv_c_w"][e], h0, h1, tiles_per_seq, seg)
            x, h = _out_proj([gated], [p["w_out_c"]], e, x, mod, tiles_per_group, tm, g_ffn)
            cs.append(st[:, -(kc - 1):] if seg is None else st.reshape(b, t, d)[:, -(kc - 1):])
        x = _ffn(h, x, mod, tiles_per_group, tm, _ff_tile(p["w_gate"].shape[2]),
                 p["w_gate"], p["w_up"], p["w_down"], l, p["final_g"], final=(l == depth - 1))
    k_all, v_all = kv
    return (x.reshape(b, t, d), k_all.reshape(n_even, b, t, ha, dh), v_all.reshape(n_even, b, t, ha, dh),
            jnp.stack(fs), jnp.stack(bs), jnp.stack(cs))


def kernel(x_prompt, x_sample, cache_k, cache_v, cache_logf, state_convb, state_convc, c_prompt, c_sample,
           ada_w, ada_b, norm_mix_g, norm_ffn_g, w_in_ab, b_f, dw_b_w, dw_b_bias, ln_b_g, ln_b_b, w_out_ab,
           w_in_c, conv_c_w, w_out_c, w_gate, w_up, w_down, final_g):
    bp, tp, d = x_prompt.shape
    bs_, ts, _ = x_sample.shape
    depth = ada_w.shape[0]
    ha = b_f.shape[1]
    d_in = w_in_ab.shape[2]
    da = d_in - ha - 2 * w_out_ab.shape[1]
    db = w_out_ab.shape[1] - da
    dh = da // ha
    assert conv_c_w.shape[1] == 3 and tp % SUBLANES == 0 and (bs_ * ts) % SUBLANES == 0

    bf = lambda a: a.astype(BF16)
    p = {
        "ha": ha, "dh": dh, "da": da, "db": db,
        "norm_mix_g": norm_mix_g.reshape(depth, 1, d), "norm_ffn_g": norm_ffn_g.reshape(depth, 1, d),
        "wqkv": bf(w_in_ab[:, :, 0:3 * da]),
        "wf": bf(w_in_ab[:, :, 3 * da:3 * da + ha].transpose(0, 2, 1)),
        "bf": b_f.reshape(-1, ha, 1),
        "wg_ab": bf(w_in_ab[:, :, 3 * da + ha:]),
        "dw_b_w": dw_b_w, "dw_b_bias": dw_b_bias.reshape(-1, 1, db),
        "ln_b_g": ln_b_g.reshape(-1, 1, db), "ln_b_b": ln_b_b.reshape(-1, 1, db),
        "wout_a": bf(w_out_ab[:, 0:da]), "wout_b": bf(w_out_ab[:, da:]),
        "w_in_c": bf(w_in_c), "conv_c_w": conv_c_w, "w_out_c": bf(w_out_c),
        "w_gate": bf(w_gate), "w_up": bf(w_up), "w_down": bf(w_down),
        "final_g": final_g.reshape(1, d),
    }

    nc = bp + bs_
    nc_pad = -(-nc // SUBLANES) * SUBLANES
    c_all = jnp.pad(jnp.concatenate([c_prompt, c_sample], axis=0), ((0, nc_pad - nc), (0, 0)))
    mod_all = _ada(c_all, ada_w, ada_b)
    mod_p = mod_all[:, 0:bp].reshape(depth, bp, 1, 6 * d)
    mod_s = jnp.repeat(mod_all[:, bp:nc], ts, axis=1).reshape(depth, 1, bs_ * ts, 6 * d)

    cache = {"k": cache_k, "v": cache_v, "logf": cache_logf, "convb": state_convb, "convc": state_convc}
    y_p, k_p, v_p, f_p, b_p, c_p = _trunk(x_prompt, mod_p, p, None)
    y_s, k_s, v_s, f_s, b_s, c_s = _trunk(x_sample, mod_s, p, cache)
    return (y_p, y_s, k_p, v_p, f_p, b_p, c_p, k_s, v_s, f_s, b_s, c_s)
```

```python
import functools
import math

import jax
import jax.numpy as jnp
from jax import lax
from jax.experimental import pallas as pl
from jax.experimental.pallas import tpu as pltpu

RMS_EPS = 1e-6
LN_EPS = 1e-5
LOG2E = math.log2(math.e)

V7X_VMEM_BYTES = 64 * 1024 * 1024
VMEM_LIMIT = V7X_VMEM_BYTES - 2 * 1024 * 1024
LANES = 128
SUBLANES = 8
CONV_HALO = 32

F32 = jnp.float32
BF16 = jnp.bfloat16

SH1, SC1, G1, SH2, SC2, G2 = range(6)


def _cparams(n_axes):
    return pltpu.CompilerParams(dimension_semantics=("arbitrary",) * n_axes, vmem_limit_bytes=VMEM_LIMIT)


def _resident(shape):
    nd = len(shape)
    return pl.BlockSpec(shape, lambda *_: (0,) * nd, pipeline_mode=pl.Buffered(1))


def _resident_layer(stack, layer):
    nd = stack.ndim - 1
    return pl.BlockSpec((None,) + stack.shape[1:], lambda *_: (layer,) + (0,) * nd, pipeline_mode=pl.Buffered(1))


def _mod_spec(mod, col, tiles_per_group, d):
    mr = mod.shape[1]
    return pl.BlockSpec((None, mr, d), lambda i, *_: (i // tiles_per_group, 0, col))


def _sigmoid(x):
    return 1.0 / (1.0 + jnp.exp(-x))


def _silu(x):
    return x * _sigmoid(x)


def _rms(x):
    return x * lax.rsqrt(jnp.mean(x * x, axis=-1, keepdims=True) + RMS_EPS)


def _modulate(x, g, shift, scale):
    return (_rms(x) * g) * (1.0 + scale) + shift


def _dot(a, b):
    return jnp.dot(a, b, preferred_element_type=F32)


def _dot_nt(a, b):
    return lax.dot_general(a, b, (((1,), (1,)), ((), ())), preferred_element_type=F32)


def _ada_kernel(c_ref, w_ref, b_ref, o_ref):
    a = _silu(c_ref[...]).astype(BF16)
    o_ref[...] = _dot(a, w_ref[...].astype(BF16)) + b_ref[...]


def _ada(c_all, ada_w, ada_b):
    depth, d, n = ada_w.shape
    r = c_all.shape[0]
    tn = 1024 if n % 1024 == 0 else n
    return pl.pallas_call(
        _ada_kernel,
        grid=(depth, n // tn),
        in_specs=[
            pl.BlockSpec((r, d), lambda l, j: (0, 0)),
            pl.BlockSpec((None, d, tn), lambda l, j: (l, 0, j)),
            pl.BlockSpec((None, 1, tn), lambda l, j: (l, 0, j)),
        ],
        out_specs=pl.BlockSpec((None, r, tn), lambda l, j: (l, 0, j)),
        out_shape=jax.ShapeDtypeStruct((depth, r, n), F32),
        compiler_params=_cparams(2),
        name="ada",
    )(c_all, ada_w, ada_b.reshape(depth, 1, n))


def _in_ab_kernel(*refs, da, db, qscale, n_prev, e):
    x_ref, sh_ref, sc_ref, g_ref, wqkv_ref, wf_ref, bf_ref, wg_ref = refs[:8]
    q_ref, k_ref, v_ref, u_ref, lf_ref = refs[8 + n_prev:]
    h = _modulate(x_ref[...], g_ref[...], sh_ref[...], sc_ref[...]).astype(BF16)
    q_ref[...] = (_dot(h, wqkv_ref[:, 0:da]) * qscale).astype(BF16)
    if n_prev:
        k_ref[...] = _dot(h, wqkv_ref[:, da:2 * da])
        v_ref[...] = _dot(h, wqkv_ref[:, 2 * da:3 * da])
    else:
        for o_ref, c0 in ((k_ref, da), (v_ref, 2 * da)):
            for e2 in range(o_ref.shape[0]):
                if e2 == e:
                    o_ref[e2] = _dot(h, wqkv_ref[:, c0:c0 + da])
                else:
                    o_ref[e2] = jnp.zeros(o_ref.shape[1:], F32)
    ga = _dot(h, wg_ref[:, 0:db])
    gb = _dot(h, wg_ref[:, db:2 * db])
    u_ref[...] = ga * _sigmoid(gb)
    fz = _dot_nt(wf_ref[...], h) + bf_ref[...]
    lf_ref[...] = jnp.minimum(fz, 0.0) - jnp.log1p(jnp.exp(-jnp.abs(fz)))


def _in_ab(x, mod, tiles_per_group, tm, g, wqkv, wf, bf, wg, e, n_even, kv_prev, dh):
    r, d = x.shape
    da = wqkv.shape[2] // 3
    db = wg.shape[2] // 2
    ha = wf.shape[1]
    row = lambda w: pl.BlockSpec((tm, w), lambda i: (i, 0))
    kv_shape = jax.ShapeDtypeStruct((n_even, r, da), F32)
    n_prev = 0 if kv_prev is None else 2
    if n_prev:
        kv_spec = pl.BlockSpec((None, tm, da), lambda i: (e, i, 0))
    else:
        kv_spec = pl.BlockSpec((n_even, tm, da), lambda i: (0, i, 0))
    outs = pl.pallas_call(
        functools.partial(_in_ab_kernel, da=da, db=db, qscale=LOG2E * float(dh) ** -0.5, n_prev=n_prev, e=e),
        grid=(r // tm,),
        in_specs=[
            row(d),
            _mod_spec(mod, SH1, tiles_per_group, d),
            _mod_spec(mod, SC1, tiles_per_group, d),
            _resident((1, d)),
            _resident_layer(wqkv, e), _resident_layer(wf, e), _resident_layer(bf, e), _resident_layer(wg, e),
        ] + [pl.BlockSpec(memory_space=pl.ANY)] * n_prev,
        out_specs=[row(da), kv_spec, kv_spec, row(db), pl.BlockSpec((ha, tm), lambda i: (0, i))],
        out_shape=[
            jax.ShapeDtypeStruct((r, da), BF16),
            kv_shape, kv_shape,
            jax.ShapeDtypeStruct((r, db), F32),
            jax.ShapeDtypeStruct((ha, r), F32),
        ],
        input_output_aliases={8: 1, 9: 2} if n_prev else {},
        compiler_params=_cparams(1),
        name="in_ab",
    )(x, mod, mod, g, wqkv, wf, bf, wg, *(kv_prev or ()))
    return outs


def _cumsum_kernel(x_ref, o_ref):
    x = x_ref[...]
    t = x.shape[-1]
    lane = lax.broadcasted_iota(jnp.int32, x.shape, 1)
    s = 1
    while s < t:
        x = x + jnp.where(lane >= s, pltpu.roll(x, s, axis=1), 0.0)
        s *= 2
    o_ref[...] = x


def _cumsum(logf_t, nb, t):
    ha = logf_t.shape[0]
    return pl.pallas_call(
        _cumsum_kernel,
        grid=(nb,),
        in_specs=[pl.BlockSpec((ha, t), lambda b: (0, b))],
        out_specs=pl.BlockSpec((None, ha, t), lambda b: (b, 0, 0)),
        out_shape=jax.ShapeDtypeStruct((nb, ha, t), F32),
        compiler_params=_cparams(1),
        name="cumsum",
    )(logf_t)


def _attn_prompt_kernel(q_ref, k_ref, v_ref, l_ref, o_ref, kt_scr, va_scr, s_scr, p_scr, *, blk, nblk, dh):
    t = nblk * blk
    nl = blk // LANES
    kt_scr[...] = k_ref[...].T.astype(BF16)
    va_scr[:, 0:dh] = v_ref[...].astype(BF16)
    va_scr[:, dh:2 * dh] = jnp.ones((t, dh), BF16)
    row = lax.broadcasted_iota(jnp.int32, (blk, blk), 0)
    col = lax.broadcasted_iota(jnp.int32, (blk, blk), 1)

    for i in range(nblk):
        buf = i % 2
        q = q_ref[i * blk:(i + 1) * blk, :]
        mx = None
        for c in range(i + 1):
            cs = slice(c * blk, (c + 1) * blk)
            s = _dot(q, kt_scr[:, cs]) - l_ref[:, cs] * LOG2E
            if c == i:
                s = jnp.where(col <= row, s, -jnp.inf)
            s_scr[buf, :, cs] = s
            for l in range(nl):
                sl = s[:, l * LANES:(l + 1) * LANES]
                mx = sl if mx is None else jnp.maximum(mx, sl)
        mb = jnp.broadcast_to(jnp.max(mx, axis=-1, keepdims=True), (blk, LANES))
        for l in range((i + 1) * nl):
            ls = slice(l * LANES, (l + 1) * LANES)
            p_scr[buf, :, ls] = jnp.exp2(s_scr[buf, :, ls] - mb).astype(BF16)
        vis = (i + 1) * blk
        acc = _dot(p_scr[buf, :, 0:vis], va_scr[0:vis, :])
        o_ref[i * blk:(i + 1) * blk, :] = (acc[:, 0:dh] / acc[:, dh:2 * dh]).astype(o_ref.dtype)


def _attn_prompt(q, k, v, e, cum, b, t, ha, dh, blk):
    assert dh == LANES and blk % LANES == 0 and t % blk == 0
    nblk = t // blk
    head = pl.BlockSpec((t, dh), lambda bi, h: (bi, h))
    kv_head = pl.BlockSpec((None, t, dh), lambda bi, h: (e, bi, h))
    return pl.pallas_call(
        functools.partial(_attn_prompt_kernel, blk=blk, nblk=nblk, dh=dh),
        grid=(b, ha),
        in_specs=[head, kv_head, kv_head,
                  pl.BlockSpec((None, None, 1, t), lambda bi, h: (bi, h, 0, 0))],
        out_specs=head,
        out_shape=jax.ShapeDtypeStruct((b * t, ha * dh), BF16),
        scratch_shapes=[
            pltpu.VMEM((dh, t), BF16),
            pltpu.VMEM((t, 2 * dh), BF16),
            pltpu.VMEM((2, blk, t), F32),
            pltpu.VMEM((2, blk, t), BF16),
        ],
        compiler_params=_cparams(2),
        name="attn_prompt",
    )(q, k, v, cum.reshape(b, ha, 1, t))


def _attn_sample_kernel(q_ref, kn_ref, vn_ref, ck_ref, cv_ref, lc_ref, ln_ref, o_ref, *, ha, dh, past, npad):
    s_new = q_ref.shape[0]
    n = ha * s_new
    heads = [slice(h * dh, (h + 1) * dh) for h in range(ha)]
    qs = jnp.concatenate([q_ref[:, hs] for hs in heads], axis=0)
    ck = ck_ref[...].reshape(past * ha, dh).astype(BF16)
    cv = cv_ref[...].reshape(past * ha, dh).astype(BF16)
    row_head = lax.broadcasted_iota(jnp.int32, (ha, s_new, past * ha), 0).reshape(n, past * ha)
    col_head = lax.rem(lax.broadcasted_iota(jnp.int32, (n, past * ha), 1), ha)
    sc = jnp.where(col_head == row_head, _dot_nt(qs, ck) - lc_ref[...] * LOG2E, -jnp.inf)
    ln = jnp.concatenate([jnp.broadcast_to(ln_ref[h:h + 1, :], (s_new, npad)) for h in range(ha)], axis=0)
    sn = jnp.concatenate([_dot_nt(q_ref[:, hs], kn_ref[:, hs].astype(BF16)) for hs in heads], axis=0)
    pos = lax.broadcasted_iota(jnp.int32, (ha, s_new, npad), 1).reshape(n, npad)
    col = lax.broadcasted_iota(jnp.int32, (n, npad), 1)
    sn = jnp.where(col <= pos, sn - ln * LOG2E, -jnp.inf)
    m = jnp.maximum(jnp.max(sc, axis=-1, keepdims=True), jnp.max(sn, axis=-1, keepdims=True))
    pc = jnp.exp2(sc - m)
    pn = jnp.exp2(sn - m)
    denom = jnp.sum(pc, axis=-1, keepdims=True) + jnp.sum(pn, axis=-1, keepdims=True)
    acc = _dot(pc.astype(BF16), cv)
    pn = pn.astype(BF16)
    for h, hs in enumerate(heads):
        rs = slice(h * s_new, (h + 1) * s_new)
        o = acc[rs] + _dot(pn[rs], vn_ref[:, hs].astype(BF16))
        o_ref[:, hs] = (o / denom[rs]).astype(o_ref.dtype)


def _attn_sample(q, kn, vn, cache_k, cache_v, layer, cum):
    b, s, da = q.shape
    npad = kn.shape[1]
    _, _, past, ha, dh = cache_k.shape
    assert s % SUBLANES == 0
    cum_cache = cum[:, :, 0:past].transpose(0, 2, 1).reshape(b, 1, past * ha)
    cum_new = cum[:, :, past:past + npad]
    per_b = lambda rows: pl.BlockSpec((None, rows, da), lambda bi: (bi, 0, 0))
    cache = pl.BlockSpec((None, None, past, ha, dh), lambda bi: (layer, bi, 0, 0, 0))
    return pl.pallas_call(
        functools.partial(_attn_sample_kernel, ha=ha, dh=dh, past=past, npad=npad),
        grid=(b,),
        in_specs=[per_b(s), per_b(npad), per_b(npad), cache, cache,
                  pl.BlockSpec((None, 1, past * ha), lambda bi: (bi, 0, 0)),
                  pl.BlockSpec((None, ha, npad), lambda bi: (bi, 0, 0))],
        out_specs=per_b(s),
        out_shape=jax.ShapeDtypeStruct((b, s, da), BF16),
        compiler_params=_cparams(1),
        name="attn_sample",
    )(q, kn, vn, cache_k, cache_v, cum_cache, cum_new)


def _convb_kernel(u_ref, hist_ref, w_ref, b_ref, g_ref, beta_ref, z_ref, ext_scr, zs_scr, y_scr,
                  *, kb, tm, rc, tiles_per_seq):
    i = pl.program_id(1)
    c = u_ref.shape[-1]

    @pl.when(i == 0)
    def _():
        ext_scr[0:CONV_HALO, :] = hist_ref[...]

    ext_scr[CONV_HALO:CONV_HALO + tm, :] = u_ref[...]
    off = CONV_HALO - (kb - 1)
    for r0 in range(0, tm, rc):
        for c0 in range(0, c, LANES):
            cl = slice(c0, c0 + LANES)
            y = None
            for s in range(SUBLANES):
                rows = rc if s == 0 else rc + SUBLANES
                zs = None
                for j in range(off, off + kb):
                    if j % SUBLANES != s:
                        continue
                    term = w_ref[j - off:j - off + 1, cl] * ext_scr[r0 + j - s:r0 + j - s + rows, cl]
                    zs = term if zs is None else zs + term
                if zs is None:
                    continue
                if s == 0:
                    shifted = zs
                else:
                    zs_scr[s, :, :] = zs
                    shifted = zs_scr[s, s:s + rc, :]
                y = shifted if y is None else y + shifted
            y_scr[r0:r0 + rc, cl] = y
    y = y_scr[...] + b_ref[...]
    mu = jnp.mean(y, axis=-1, keepdims=True)
    yc = y - mu
    var = jnp.mean(yc * yc, axis=-1, keepdims=True)
    z = yc * lax.rsqrt(var + LN_EPS) * g_ref[...] + beta_ref[...]
    z_ref[...] = _silu(z).astype(z_ref.dtype)
    if tiles_per_seq > 1:
        ext_scr[0:CONV_HALO, :] = ext_scr[tm:tm + CONV_HALO, :]


def _convb(u, hist_pad, w, bias, g, beta, tm):
    b, t, c = u.shape
    kb = w.shape[0]
    assert kb - 1 <= CONV_HALO and t % tm == 0 and (tm >= CONV_HALO or t == tm)
    rc = min(tm, 64)
    return pl.pallas_call(
        functools.partial(_convb_kernel, kb=kb, tm=tm, rc=rc, tiles_per_seq=t // tm),
        grid=(b, t // tm),
        in_specs=[
            pl.BlockSpec((None, tm, c), lambda bi, i: (bi, i, 0)),
            pl.BlockSpec((None, CONV_HALO, c), lambda bi, i: (bi, 0, 0)),
            _resident(w.shape), _resident((1, c)), _resident((1, c)), _resident((1, c)),
        ],
        out_specs=pl.BlockSpec((None, tm, c), lambda bi, i: (bi, i, 0)),
        out_shape=jax.ShapeDtypeStruct((b, t, c), BF16),
        scratch_shapes=[pltpu.VMEM((CONV_HALO + tm, c), F32),
                        pltpu.VMEM((SUBLANES, rc + SUBLANES, LANES), F32),
                        pltpu.VMEM((tm, c), F32)],
        compiler_params=_cparams(2),
        name="convb",
    )(u, hist_pad, w, bias, g, beta)


def _in_c_kernel(x_ref, sh_ref, sc_ref, g_ref, w_ref, cw_ref, h0_ref, h1_ref, o_ref, st_ref, ext_scr,
                 *, dc, tm, tiles_per_seq, seg):
    h = _modulate(x_ref[...], g_ref[...], sh_ref[...], sc_ref[...]).astype(BF16)
    cx = _dot(h, w_ref[:, dc:2 * dc]) * _dot(h, w_ref[:, 2 * dc:3 * dc])
    if seg is None:
        i = pl.program_id(0) % tiles_per_seq

        @pl.when(i == 0)
        def _():
            ext_scr[SUBLANES - 2:SUBLANES - 1, :] = h0_ref[...]
            ext_scr[SUBLANES - 1:SUBLANES, :] = h1_ref[...]
    else:
        ext_scr[0:SUBLANES, :] = jnp.zeros((SUBLANES, dc), F32)

    ext_scr[SUBLANES:SUBLANES + tm, :] = cx
    m1 = ext_scr[SUBLANES - 1:SUBLANES - 1 + tm, :]
    m2 = ext_scr[SUBLANES - 2:SUBLANES - 2 + tm, :]
    if seg is not None:
        pos = lax.broadcasted_iota(jnp.int32, (tm, 1), 0) % seg
        m1 = jnp.where(pos == 0, h1_ref[...], m1)
        m2 = jnp.where(pos == 0, h0_ref[...], jnp.where(pos == 1, h1_ref[...], m2))
    uc = cw_ref[0:1, :] * m2 + cw_ref[1:2, :] * m1 + cw_ref[2:3, :] * cx
    o_ref[...] = (_dot(h, w_ref[:, 0:dc]) * uc).astype(o_ref.dtype)
    if seg is None:
        st_ref[...] = ext_scr[tm:tm + SUBLANES, :]
        if tiles_per_seq > 1:
            ext_scr[0:SUBLANES, :] = ext_scr[tm:tm + SUBLANES, :]
    else:
        st_ref[...] = cx


def _in_c(x, mod, tiles_per_group, tm, g, w, e, cw, h0, h1, tiles_per_seq, seg):
    r, d = x.shape
    dc = w.shape[2] // 3
    row = lambda width: pl.BlockSpec((tm, width), lambda i: (i, 0))
    if seg is None:
        nseq = r // (tm * tiles_per_seq)
        hspec = pl.BlockSpec((None, 1, dc), lambda i: (i // tiles_per_seq, 0, 0))
        st_spec = pl.BlockSpec((None, SUBLANES, dc), lambda i: (i // tiles_per_seq, 0, 0))
        st_shape = jax.ShapeDtypeStruct((nseq, SUBLANES, dc), F32)
    else:
        hspec = row(dc)
        st_spec = row(dc)
        st_shape = jax.ShapeDtypeStruct((r, dc), F32)
    return pl.pallas_call(
        functools.partial(_in_c_kernel, dc=dc, tm=tm, tiles_per_seq=tiles_per_seq, seg=seg),
        grid=(r // tm,),
        in_specs=[
            row(d),
            _mod_spec(mod, SH1, tiles_per_group, d),
            _mod_spec(mod, SC1, tiles_per_group, d),
            _resident((1, d)), _resident_layer(w, e), _resident(cw.shape), hspec, hspec,
        ],
        out_specs=[row(dc), st_spec],
        out_shape=[jax.ShapeDtypeStruct((r, dc), BF16), st_shape],
        scratch_shapes=[pltpu.VMEM((SUBLANES + tm, dc), F32)],
        compiler_params=_cparams(1),
        name="in_c",
    )(x, mod, mod, g, w, cw, h0, h1)


def _out_kernel(*refs, n_in):
    a_refs, w_refs = refs[:n_in], refs[n_in:2 * n_in]
    x_ref, gate_ref, sh_ref, sc_ref, g_ref, o_ref, h_ref = refs[2 * n_in:]
    y = _dot(a_refs[0][...], w_refs[0][...])
    for a_ref, w_ref in zip(a_refs[1:], w_refs[1:]):
        y = y + _dot(a_ref[...], w_ref[...])
    xn = x_ref[...] + gate_ref[...] * y
    o_ref[...] = xn
    h_ref[...] = _modulate(xn, g_ref[...], sh_ref[...], sc_ref[...]).astype(h_ref.dtype)


def _out_proj(acts, weights, e, x, mod, tiles_per_group, tm, g_ffn):
    r, d = x.shape
    row = lambda w: pl.BlockSpec((tm, w), lambda i: (i, 0))
    return pl.pallas_call(
        functools.partial(_out_kernel, n_in=len(acts)),
        grid=(r // tm,),
        in_specs=[row(a.shape[1]) for a in acts] + [_resident_layer(w, e) for w in weights]
        + [row(d), _mod_spec(mod, G1, tiles_per_group, d), _mod_spec(mod, SH2, tiles_per_group, d),
           _mod_spec(mod, SC2, tiles_per_group, d), _resident((1, d))],
        out_specs=[row(d), row(d)],
        out_shape=[jax.ShapeDtypeStruct((r, d), F32), jax.ShapeDtypeStruct((r, d), BF16)],
        compiler_params=_cparams(1),
        name="out_proj",
    )(*acts, *weights, x, mod, mod, mod, g_ffn)


def _ffn_kernel(h_ref, x_hbm, gate_ref, wg_ref, wu_ref, wd_ref, fg_ref, o_ref, x_buf, x_sem, *, final, tm):
    i, j = pl.program_id(0), pl.program_id(1)

    def x_copy():
        return pltpu.make_async_copy(x_hbm.at[pl.ds(pl.multiple_of(i * tm, tm), tm), :], x_buf, x_sem)

    @pl.when(j == 0)
    def _():
        x_copy().start()
        o_ref[...] = jnp.zeros(o_ref.shape, F32)

    h = h_ref[...]
    act = (_silu(_dot(h, wg_ref[...])) * _dot(h, wu_ref[...])).astype(BF16)
    o_ref[...] += _dot(act, wd_ref[...])

    @pl.when(j == pl.num_programs(1) - 1)
    def _():
        x_copy().wait()
        xn = x_buf[...] + gate_ref[...] * o_ref[...]
        if final:
            xn = _rms(xn) * fg_ref[...]
        o_ref[...] = xn


def _ffn(h, x, mod, tiles_per_group, tm, tf, wg, wu, wd, layer, final_g, final):
    r, d = x.shape
    f = wg.shape[2]
    row = pl.BlockSpec((tm, d), lambda i, j: (i, 0))
    return pl.pallas_call(
        functools.partial(_ffn_kernel, final=final, tm=tm),
        grid=(r // tm, f // tf),
        in_specs=[
            row, pl.BlockSpec(memory_space=pl.ANY),
            _mod_spec(mod, G2, tiles_per_group, d),
            pl.BlockSpec((None, d, tf), lambda i, j: (layer, 0, j)),
            pl.BlockSpec((None, d, tf), lambda i, j: (layer, 0, j)),
            pl.BlockSpec((None, tf, d), lambda i, j: (layer, j, 0)),
            _resident((1, d)),
        ],
        out_specs=row,
        out_shape=jax.ShapeDtypeStruct((r, d), F32),
        scratch_shapes=[pltpu.VMEM((tm, d), F32), pltpu.SemaphoreType.DMA(())],
        compiler_params=_cparams(2),
        name="ffn",
    )(h, x, mod, wg, wu, wd, final_g)


def _row_tile(t, cap):
    tm = min(t, cap)
    while t % tm:
        tm -= SUBLANES
    return tm


def _ff_tile(f, cap=512):
    tf = min(f, cap)
    while f % tf or tf % LANES:
        tf -= LANES
    return tf


def _trunk(x3, mods, p, cache):
    b, t, d = x3.shape
    depth = mods.shape[0]
    n_even = (depth + 1) // 2
    ha, dh, da, db = p["ha"], p["dh"], p["da"], p["db"]
    kb, kc = p["dw_b_w"].shape[1], p["conv_c_w"].shape[1]
    r = b * t
    x = x3.reshape(r, d)
    if cache is None:
        tm = _row_tile(t, 512)
        tiles_per_group = t // tm
        tiles_per_seq, seg = t // tm, None
        tm_ffn = _row_tile(t, 1024)
    else:
        tm = r
        tiles_per_group = 1
        tiles_per_seq, seg = 1, t
        tm_ffn = r
    kv = None
    fs, bs, cs = [], [], []
    for l in range(depth):
        mod = mods[l]
        e = l // 2
        g_ffn = p["norm_ffn_g"][l]
        if l % 2 == 0:
            q, k_all, v_all, u, lf_t = _in_ab(x, mod, tiles_per_group, tm, p["norm_mix_g"][l], p["wqkv"],
                                              p["wf"], p["bf"], p["wg_ab"], e, n_even, kv, dh)
            kv = (k_all, v_all)
            u3 = u.reshape(b, t, db)
            if cache is None:
                cum = _cumsum(lf_t, b, t)
                att = _attn_prompt(q, k_all, v_all, e, cum, b, t, ha, dh, _row_tile(t, 512))
                hist = jnp.zeros((b, kb - 1, db), F32)
            else:
                past = cache["k"].shape[2]
                npad = LANES
                lf_new = lf_t.reshape(ha, b, t).transpose(1, 0, 2)
                lf_all = jnp.concatenate([cache["logf"][e].transpose(0, 2, 1), lf_new,
                                          jnp.zeros((b, ha, npad - t), F32)], axis=-1)
                cum = _cumsum(lf_all.transpose(1, 0, 2).reshape(ha, b * (past + npad)), b, past + npad)
                pad = lambda a: jnp.pad(a[e].reshape(b, t, da), ((0, 0), (0, npad - t), (0, 0)))
                att = _attn_sample(q.reshape(b, t, da), pad(k_all), pad(v_all), cache["k"], cache["v"], e,
                                   cum).reshape(r, da)
                hist = cache["convb"][e]
            hist_pad = jnp.pad(hist, ((0, 0), (CONV_HALO - (kb - 1), 0), (0, 0)))
            z = _convb(u3, hist_pad, p["dw_b_w"][e], p["dw_b_bias"][e], p["ln_b_g"][e], p["ln_b_b"][e],
                       _row_tile(t, 256)).reshape(r, db)
            x, h = _out_proj([att, z], [p["wout_a"], p["wout_b"]], e, x, mod, tiles_per_group, tm, g_ffn)
            fs.append(lf_t.reshape(ha, b, t).transpose(1, 2, 0))
            keep = kb - 1
            bs.append(u3[:, t - keep:] if t >= keep else jnp.concatenate([hist[:, t:], u3], axis=1))
        else:
            hist = jnp.zeros((b, kc - 1, d), F32) if cache is None else cache["convc"][e]
            if seg is None:
                h0, h1 = hist[:, 0:1], hist[:, 1:2]
            else:
                h0, h1 = jnp.repeat(hist[:, 0], t, axis=0), jnp.repeat(hist[:, 1], t, axis=0)
            gated, st = _in_c(x, mod, tiles_per_group, tm, p["norm_mix_g"][l], p["w_in_c"], e,
                              p["conv_c_w"][e], h0, h1, tiles_per_seq, seg)
            x, h = _out_proj([gated], [p["w_out_c"]], e, x, mod, tiles_per_group, tm, g_ffn)
            cs.append(st[:, -(kc - 1):] if seg is None else st.reshape(b, t, d)[:, -(kc - 1):])
        x = _ffn(h, x, mod, tiles_per_group * tm // tm_ffn, tm_ffn, _ff_tile(p["w_gate"].shape[2]),
                 p["w_gate"], p["w_up"], p["w_down"], l, p["final_g"], final=(l == depth - 1))
    k_all, v_all = kv
    return (x.reshape(b, t, d), k_all.reshape(n_even, b, t, ha, dh), v_all.reshape(n_even, b, t, ha, dh),
            jnp.stack(fs), jnp.stack(bs), jnp.stack(cs))


def kernel(x_prompt, x_sample, cache_k, cache_v, cache_logf, state_convb, state_convc, c_prompt, c_sample,
           ada_w, ada_b, norm_mix_g, norm_ffn_g, w_in_ab, b_f, dw_b_w, dw_b_bias, ln_b_g, ln_b_b, w_out_ab,
           w_in_c, conv_c_w, w_out_c, w_gate, w_up, w_down, final_g):
    bp, tp, d = x_prompt.shape
    bs_, ts, _ = x_sample.shape
    depth = ada_w.shape[0]
    ha = b_f.shape[1]
    d_in = w_in_ab.shape[2]
    da = d_in - ha - 2 * w_out_ab.shape[1]
    db = w_out_ab.shape[1] - da
    dh = da // ha
    assert conv_c_w.shape[1] == 3 and tp % SUBLANES == 0 and (bs_ * ts) % SUBLANES == 0

    bf = lambda a: a.astype(BF16)
    p = {
        "ha": ha, "dh": dh, "da": da, "db": db,
        "norm_mix_g": norm_mix_g.reshape(depth, 1, d), "norm_ffn_g": norm_ffn_g.reshape(depth, 1, d),
        "wqkv": bf(w_in_ab[:, :, 0:3 * da]),
        "wf": bf(w_in_ab[:, :, 3 * da:3 * da + ha].transpose(0, 2, 1)),
        "bf": b_f.reshape(-1, ha, 1),
        "wg_ab": bf(w_in_ab[:, :, 3 * da + ha:]),
        "dw_b_w": dw_b_w, "dw_b_bias": dw_b_bias.reshape(-1, 1, db),
        "ln_b_g": ln_b_g.reshape(-1, 1, db), "ln_b_b": ln_b_b.reshape(-1, 1, db),
        "wout_a": bf(w_out_ab[:, 0:da]), "wout_b": bf(w_out_ab[:, da:]),
        "w_in_c": bf(w_in_c), "conv_c_w": conv_c_w, "w_out_c": bf(w_out_c),
        "w_gate": bf(w_gate), "w_up": bf(w_up), "w_down": bf(w_down),
        "final_g": final_g.reshape(1, d),
    }

    nc = bp + bs_
    nc_pad = -(-nc // SUBLANES) * SUBLANES
    c_all = jnp.pad(jnp.concatenate([c_prompt, c_sample], axis=0), ((0, nc_pad - nc), (0, 0)))
    mod_all = _ada(c_all, ada_w, ada_b)
    mod_p = mod_all[:, 0:bp].reshape(depth, bp, 1, 6 * d)
    mod_s = jnp.repeat(mod_all[:, bp:nc], ts, axis=1).reshape(depth, 1, bs_ * ts, 6 * d)

    cache = {"k": cache_k, "v": cache_v, "logf": cache_logf, "convb": state_convb, "convc": state_convc}
    y_p, k_p, v_p, f_p, b_p, c_p = _trunk(x_prompt, mod_p, p, None)
    y_s, k_s, v_s, f_s, b_s, c_s = _trunk(x_sample, mod_s, p, cache)
    return (y_p, y_s, k_p, v_p, f_p, b_p, c_p, k_s, v_s, f_s, b_s, c_s)
```

```python
import functools
import math

import jax
import jax.numpy as jnp
from jax import lax
from jax.experimental import pallas as pl
from jax.experimental.pallas import tpu as pltpu

RMS_EPS = 1e-6
LN_EPS = 1e-5
LOG2E = math.log2(math.e)

V7X_VMEM_BYTES = 64 * 1024 * 1024
VMEM_LIMIT = V7X_VMEM_BYTES - 2 * 1024 * 1024
LANES = 128
SUBLANES = 8
CONV_HALO = 32

F32 = jnp.float32
BF16 = jnp.bfloat16

SH1, SC1, G1, SH2, SC2, G2 = range(6)


def _cparams(n_axes):
    return pltpu.CompilerParams(dimension_semantics=("arbitrary",) * n_axes, vmem_limit_bytes=VMEM_LIMIT)


def _resident(shape):
    nd = len(shape)
    return pl.BlockSpec(shape, lambda *_: (0,) * nd, pipeline_mode=pl.Buffered(1))


def _resident_layer(stack, layer):
    nd = stack.ndim - 1
    return pl.BlockSpec((None,) + stack.shape[1:], lambda *_: (layer,) + (0,) * nd, pipeline_mode=pl.Buffered(1))


def _mod_spec(mod, col, tiles_per_group, d):
    _, layer, mr = mod
    return pl.BlockSpec((None, None, mr, d), lambda i, *_: (layer, i // tiles_per_group, 0, col))


def _sigmoid(x):
    return 1.0 / (1.0 + jnp.exp(-x))


def _silu(x):
    return x * _sigmoid(x)


def _rms(x):
    return x * lax.rsqrt(jnp.mean(x * x, axis=-1, keepdims=True) + RMS_EPS)


def _modulate(x, g, shift, scale):
    return (_rms(x) * g) * (1.0 + scale) + shift


def _dot(a, b):
    return jnp.dot(a, b, preferred_element_type=F32)


def _dot_nt(a, b):
    return lax.dot_general(a, b, (((1,), (1,)), ((), ())), preferred_element_type=F32)


def _ada_kernel(c_ref, w_ref, b_ref, o_ref):
    a = _silu(c_ref[...]).astype(BF16)
    o_ref[...] = _dot(a, w_ref[...].astype(BF16)) + b_ref[...]


def _ada(c_all, ada_w, ada_b):
    depth, d, n = ada_w.shape
    r = c_all.shape[0]
    tn = 1024 if n % 1024 == 0 else n
    return pl.pallas_call(
        _ada_kernel,
        grid=(depth, n // tn),
        in_specs=[
            pl.BlockSpec((r, d), lambda l, j: (0, 0)),
            pl.BlockSpec((None, d, tn), lambda l, j: (l, 0, j)),
            pl.BlockSpec((None, 1, tn), lambda l, j: (l, 0, j)),
        ],
        out_specs=pl.BlockSpec((None, r, tn), lambda l, j: (l, 0, j)),
        out_shape=jax.ShapeDtypeStruct((depth, r, n), F32),
        compiler_params=_cparams(2),
        name="ada",
    )(c_all, ada_w, ada_b.reshape(depth, 1, n))


def _in_ab_kernel(*refs, da, db, qscale, n_prev, e):
    x_ref, sh_ref, sc_ref, g_ref, wqkv_ref, wf_ref, bf_ref, wg_ref = refs[:8]
    q_ref, k_ref, v_ref, u_ref, lf_ref = refs[8 + n_prev:]
    h = _modulate(x_ref[...], g_ref[...], sh_ref[...], sc_ref[...]).astype(BF16)
    q_ref[...] = (_dot(h, wqkv_ref[:, 0:da]) * qscale).astype(BF16)
    if n_prev:
        k_ref[...] = _dot(h, wqkv_ref[:, da:2 * da])
        v_ref[...] = _dot(h, wqkv_ref[:, 2 * da:3 * da])
    else:
        for o_ref, c0 in ((k_ref, da), (v_ref, 2 * da)):
            for e2 in range(o_ref.shape[0]):
                if e2 == e:
                    o_ref[e2] = _dot(h, wqkv_ref[:, c0:c0 + da])
                else:
                    o_ref[e2] = jnp.zeros(o_ref.shape[1:], F32)
    cb = 512 if db % 512 == 0 else db
    for c0 in range(0, db, cb):
        ga = _dot(h, wg_ref[:, c0:c0 + cb])
        gb = _dot(h, wg_ref[:, db + c0:db + c0 + cb])
        u_ref[:, c0:c0 + cb] = ga * _sigmoid(gb)
    fz = _dot_nt(wf_ref[...], h) + bf_ref[...]
    lf_ref[...] = jnp.minimum(fz, 0.0) - jnp.log1p(jnp.exp(-jnp.abs(fz)))


def _in_ab(x, mod, tiles_per_group, tm, g, wqkv, wf, bf, wg, e, n_even, kv_prev, dh):
    r, d = x.shape
    da = wqkv.shape[2] // 3
    db = wg.shape[2] // 2
    ha = wf.shape[1]
    row = lambda w: pl.BlockSpec((tm, w), lambda i: (i, 0))
    kv_shape = jax.ShapeDtypeStruct((n_even, r, da), F32)
    n_prev = 0 if kv_prev is None else 2
    if n_prev:
        kv_spec = pl.BlockSpec((None, tm, da), lambda i: (e, i, 0))
    else:
        kv_spec = pl.BlockSpec((n_even, tm, da), lambda i: (0, i, 0))
    outs = pl.pallas_call(
        functools.partial(_in_ab_kernel, da=da, db=db, qscale=LOG2E * float(dh) ** -0.5, n_prev=n_prev, e=e),
        grid=(r // tm,),
        in_specs=[
            row(d),
            _mod_spec(mod, SH1, tiles_per_group, d),
            _mod_spec(mod, SC1, tiles_per_group, d),
            _resident((1, d)),
            _resident_layer(wqkv, e), _resident_layer(wf, e), _resident_layer(bf, e), _resident_layer(wg, e),
        ] + [pl.BlockSpec(memory_space=pl.ANY)] * n_prev,
        out_specs=[row(da), kv_spec, kv_spec, row(db), pl.BlockSpec((ha, tm), lambda i: (0, i))],
        out_shape=[
            jax.ShapeDtypeStruct((r, da), BF16),
            kv_shape, kv_shape,
            jax.ShapeDtypeStruct((r, db), F32),
            jax.ShapeDtypeStruct((ha, r), F32),
        ],
        input_output_aliases={8: 1, 9: 2} if n_prev else {},
        compiler_params=_cparams(1),
        name="in_ab",
    )(x, mod[0], mod[0], g, wqkv, wf, bf, wg, *(kv_prev or ()))
    return outs


def _cumsum_kernel(x_ref, o_ref):
    x = x_ref[...]
    t = x.shape[-1]
    lane = lax.broadcasted_iota(jnp.int32, x.shape, 1)
    s = 1
    while s < t:
        x = x + jnp.where(lane >= s, pltpu.roll(x, s, axis=1), 0.0)
        s *= 2
    o_ref[...] = x


def _cumsum(logf_t, nb, t):
    ha = logf_t.shape[0]
    return pl.pallas_call(
        _cumsum_kernel,
        grid=(nb,),
        in_specs=[pl.BlockSpec((ha, t), lambda b: (0, b))],
        out_specs=pl.BlockSpec((None, ha, t), lambda b: (b, 0, 0)),
        out_shape=jax.ShapeDtypeStruct((nb, ha, t), F32),
        compiler_params=_cparams(1),
        name="cumsum",
    )(logf_t)


def _attn_prompt_kernel(q_ref, k_ref, v_ref, l_ref, o_ref, kt_scr, va_scr, s_scr, p_scr, *, blk, nblk, dh):
    t = nblk * blk
    nl = blk // LANES
    kt_scr[...] = k_ref[...].T.astype(BF16)
    va_scr[:, 0:dh] = v_ref[...].astype(BF16)
    va_scr[:, dh:2 * dh] = jnp.ones((t, dh), BF16)
    row = lax.broadcasted_iota(jnp.int32, (blk, blk), 0)
    col = lax.broadcasted_iota(jnp.int32, (blk, blk), 1)

    def logits_chunk(i, c, mx):
        cs = slice(c * blk, (c + 1) * blk)
        s = _dot(q_ref[i * blk:(i + 1) * blk, :], kt_scr[:, cs]) - l_ref[:, cs] * LOG2E
        if c == i:
            s = jnp.where(col <= row, s, -jnp.inf)
        s_scr[i % 2, :, cs] = s
        for l in range(nl):
            sl = s[:, l * LANES:(l + 1) * LANES]
            mx = sl if mx is None else jnp.maximum(mx, sl)
        return mx

    def exp_chunk(i, c, mb):
        for l in range(c * nl, (c + 1) * nl):
            ls = slice(l * LANES, (l + 1) * LANES)
            p_scr[i % 2, :, ls] = jnp.exp2(s_scr[i % 2, :, ls] - mb).astype(BF16)

    def row_max(mx):
        return jnp.broadcast_to(jnp.max(mx, axis=-1, keepdims=True), (blk, LANES))

    mb = row_max(logits_chunk(0, 0, None))
    for i in range(nblk):
        mx = None
        for c in range(i + 2):
            if i + 1 < nblk:
                mx = logits_chunk(i + 1, c, mx)
            if c <= i:
                exp_chunk(i, c, mb)
        vis = (i + 1) * blk
        acc = _dot(p_scr[i % 2, :, 0:vis], va_scr[0:vis, :])
        o_ref[i * blk:(i + 1) * blk, :] = (acc[:, 0:dh] / acc[:, dh:2 * dh]).astype(o_ref.dtype)
        if i + 1 < nblk:
            mb = row_max(mx)


def _attn_prompt(q, k, v, e, cum, b, t, ha, dh, blk):
    assert dh == LANES and blk % LANES == 0 and t % blk == 0
    nblk = t // blk
    head = pl.BlockSpec((t, dh), lambda bi, h: (bi, h))
    kv_head = pl.BlockSpec((None, t, dh), lambda bi, h: (e, bi, h))
    return pl.pallas_call(
        functools.partial(_attn_prompt_kernel, blk=blk, nblk=nblk, dh=dh),
        grid=(b, ha),
        in_specs=[head, kv_head, kv_head,
                  pl.BlockSpec((None, None, 1, t), lambda bi, h: (bi, h, 0, 0))],
        out_specs=head,
        out_shape=jax.ShapeDtypeStruct((b * t, ha * dh), BF16),
        scratch_shapes=[
            pltpu.VMEM((dh, t), BF16),
            pltpu.VMEM((t, 2 * dh), BF16),
            pltpu.VMEM((2, blk, t), F32),
            pltpu.VMEM((2, blk, t), BF16),
        ],
        compiler_params=_cparams(2),
        name="attn_prompt",
    )(q, k, v, cum.reshape(b, ha, 1, t))


def _attn_sample_kernel(q_ref, kn_ref, vn_ref, ck_ref, cv_ref, lc_ref, ln_ref, o_ref, *, ha, dh, past, npad):
    s_new = q_ref.shape[0]
    n = ha * s_new
    heads = [slice(h * dh, (h + 1) * dh) for h in range(ha)]
    qs = jnp.concatenate([q_ref[:, hs] for hs in heads], axis=0)
    ck = ck_ref[...].reshape(past * ha, dh).astype(BF16)
    cv = cv_ref[...].reshape(past * ha, dh).astype(BF16)
    row_head = lax.broadcasted_iota(jnp.int32, (ha, s_new, past * ha), 0).reshape(n, past * ha)
    col_head = lax.rem(lax.broadcasted_iota(jnp.int32, (n, past * ha), 1), ha)
    sc = jnp.where(col_head == row_head, _dot_nt(qs, ck) - lc_ref[...] * LOG2E, -jnp.inf)
    ln = jnp.concatenate([jnp.broadcast_to(ln_ref[h:h + 1, :], (s_new, npad)) for h in range(ha)], axis=0)
    sn = jnp.concatenate([_dot_nt(q_ref[:, hs], kn_ref[:, hs].astype(BF16)) for hs in heads], axis=0)
    pos = lax.broadcasted_iota(jnp.int32, (ha, s_new, npad), 1).reshape(n, npad)
    col = lax.broadcasted_iota(jnp.int32, (n, npad), 1)
    sn = jnp.where(col <= pos, sn - ln * LOG2E, -jnp.inf)
    m = jnp.maximum(jnp.max(sc, axis=-1, keepdims=True), jnp.max(sn, axis=-1, keepdims=True))
    pc = jnp.exp2(sc - m)
    pn = jnp.exp2(sn - m)
    denom = jnp.sum(pc, axis=-1, keepdims=True) + jnp.sum(pn, axis=-1, keepdims=True)
    acc = _dot(pc.astype(BF16), cv)
    pn = pn.astype(BF16)
    for h, hs in enumerate(heads):
        rs = slice(h * s_new, (h + 1) * s_new)
        o = acc[rs] + _dot(pn[rs], vn_ref[:, hs].astype(BF16))
        o_ref[:, hs] = (o / denom[rs]).astype(o_ref.dtype)


def _attn_sample(q, kn, vn, cache_k, cache_v, layer, cum):
    b, s, da = q.shape
    npad = kn.shape[1]
    _, _, past, ha, dh = cache_k.shape
    assert s % SUBLANES == 0
    cum_cache = cum[:, :, 0:past].transpose(0, 2, 1).reshape(b, 1, past * ha)
    cum_new = cum[:, :, past:past + npad]
    per_b = lambda rows: pl.BlockSpec((None, rows, da), lambda bi: (bi, 0, 0))
    cache = pl.BlockSpec((None, None, past, ha, dh), lambda bi: (layer, bi, 0, 0, 0))
    return pl.pallas_call(
        functools.partial(_attn_sample_kernel, ha=ha, dh=dh, past=past, npad=npad),
        grid=(b,),
        in_specs=[per_b(s), per_b(npad), per_b(npad), cache, cache,
                  pl.BlockSpec((None, 1, past * ha), lambda bi: (bi, 0, 0)),
                  pl.BlockSpec((None, ha, npad), lambda bi: (bi, 0, 0))],
        out_specs=per_b(s),
        out_shape=jax.ShapeDtypeStruct((b, s, da), BF16),
        compiler_params=_cparams(1),
        name="attn_sample",
    )(q, kn, vn, cache_k, cache_v, cum_cache, cum_new)


def _convb_kernel(u_ref, hist_ref, w_ref, b_ref, g_ref, beta_ref, z_ref, ext_scr, zs_scr, y_scr,
                  *, kb, tm, rc, tiles_per_seq):
    i = pl.program_id(1)
    c = u_ref.shape[-1]

    @pl.when(i == 0)
    def _():
        ext_scr[0:CONV_HALO, :] = hist_ref[...]

    ext_scr[CONV_HALO:CONV_HALO + tm, :] = u_ref[...]
    off = CONV_HALO - (kb - 1)
    for r0 in range(0, tm, rc):
        for c0 in range(0, c, LANES):
            cl = slice(c0, c0 + LANES)
            y = None
            for s in range(SUBLANES):
                rows = rc if s == 0 else rc + SUBLANES
                zs = None
                for j in range(off, off + kb):
                    if j % SUBLANES != s:
                        continue
                    term = w_ref[j - off:j - off + 1, cl] * ext_scr[r0 + j - s:r0 + j - s + rows, cl]
                    zs = term if zs is None else zs + term
                if zs is None:
                    continue
                if s == 0:
                    shifted = zs
                else:
                    zs_scr[s, :, :] = zs
                    shifted = zs_scr[s, s:s + rc, :]
                y = shifted if y is None else y + shifted
            y_scr[r0:r0 + rc, cl] = y
    y = y_scr[...] + b_ref[...]
    mu = jnp.mean(y, axis=-1, keepdims=True)
    yc = y - mu
    var = jnp.mean(yc * yc, axis=-1, keepdims=True)
    z = yc * lax.rsqrt(var + LN_EPS) * g_ref[...] + beta_ref[...]
    z_ref[...] = _silu(z).astype(z_ref.dtype)
    if tiles_per_seq > 1:
        ext_scr[0:CONV_HALO, :] = ext_scr[tm:tm + CONV_HALO, :]


def _convb(u, hist_pad, w, bias, g, beta, tm):
    b, t, c = u.shape
    kb = w.shape[0]
    assert kb - 1 <= CONV_HALO and t % tm == 0 and (tm >= CONV_HALO or t == tm)
    rc = min(tm, 64)
    return pl.pallas_call(
        functools.partial(_convb_kernel, kb=kb, tm=tm, rc=rc, tiles_per_seq=t // tm),
        grid=(b, t // tm),
        in_specs=[
            pl.BlockSpec((None, tm, c), lambda bi, i: (bi, i, 0)),
            pl.BlockSpec((None, CONV_HALO, c), lambda bi, i: (bi, 0, 0)),
            _resident(w.shape), _resident((1, c)), _resident((1, c)), _resident((1, c)),
        ],
        out_specs=pl.BlockSpec((None, tm, c), lambda bi, i: (bi, i, 0)),
        out_shape=jax.ShapeDtypeStruct((b, t, c), BF16),
        scratch_shapes=[pltpu.VMEM((CONV_HALO + tm, c), F32),
                        pltpu.VMEM((SUBLANES, rc + SUBLANES, LANES), F32),
                        pltpu.VMEM((tm, c), F32)],
        compiler_params=_cparams(2),
        name="convb",
    )(u, hist_pad, w, bias, g, beta)


def _in_c_kernel(x_ref, sh_ref, sc_ref, g_ref, w_ref, cw_ref, h0_ref, h1_ref, o_ref, st_ref, ext_scr,
                 *, dc, tm, tiles_per_seq, seg, cb):
    h = _modulate(x_ref[...], g_ref[...], sh_ref[...], sc_ref[...]).astype(BF16)
    if seg is None:
        i = pl.program_id(0) % tiles_per_seq

        @pl.when(i == 0)
        def _():
            ext_scr[SUBLANES - 2:SUBLANES - 1, :] = h0_ref[...]
            ext_scr[SUBLANES - 1:SUBLANES, :] = h1_ref[...]
    else:
        ext_scr[0:SUBLANES, :] = jnp.zeros((SUBLANES, dc), F32)
        pos = lax.broadcasted_iota(jnp.int32, (tm, 1), 0) % seg

    for c0 in range(0, dc, cb):
        cl = slice(c0, c0 + cb)
        cx = _dot(h, w_ref[:, dc + c0:dc + c0 + cb]) * _dot(h, w_ref[:, 2 * dc + c0:2 * dc + c0 + cb])
        ext_scr[SUBLANES:SUBLANES + tm, cl] = cx
        m1 = ext_scr[SUBLANES - 1:SUBLANES - 1 + tm, cl]
        m2 = ext_scr[SUBLANES - 2:SUBLANES - 2 + tm, cl]
        if seg is not None:
            m1 = jnp.where(pos == 0, h1_ref[:, cl], m1)
            m2 = jnp.where(pos == 0, h0_ref[:, cl], jnp.where(pos == 1, h1_ref[:, cl], m2))
            st_ref[:, cl] = cx
        uc = cw_ref[0:1, cl] * m2 + cw_ref[1:2, cl] * m1 + cw_ref[2:3, cl] * cx
        o_ref[:, cl] = (_dot(h, w_ref[:, c0:c0 + cb]) * uc).astype(o_ref.dtype)
    if seg is None:
        st_ref[...] = ext_scr[tm:tm + SUBLANES, :]
        if tiles_per_seq > 1:
            ext_scr[0:SUBLANES, :] = ext_scr[tm:tm + SUBLANES, :]


def _in_c(x, mod, tiles_per_group, tm, g, w, e, cw, h0, h1, tiles_per_seq, seg):
    r, d = x.shape
    dc = w.shape[2] // 3
    cb = 512 if dc % 512 == 0 else dc
    row = lambda width: pl.BlockSpec((tm, width), lambda i: (i, 0))
    if seg is None:
        nseq = r // (tm * tiles_per_seq)
        hspec = pl.BlockSpec((None, 1, dc), lambda i: (i // tiles_per_seq, 0, 0))
        st_spec = pl.BlockSpec((None, SUBLANES, dc), lambda i: (i // tiles_per_seq, 0, 0))
        st_shape = jax.ShapeDtypeStruct((nseq, SUBLANES, dc), F32)
    else:
        hspec = row(dc)
        st_spec = row(dc)
        st_shape = jax.ShapeDtypeStruct((r, dc), F32)
    return pl.pallas_call(
        functools.partial(_in_c_kernel, dc=dc, tm=tm, tiles_per_seq=tiles_per_seq, seg=seg, cb=cb),
        grid=(r // tm,),
        in_specs=[
            row(d),
            _mod_spec(mod, SH1, tiles_per_group, d),
            _mod_spec(mod, SC1, tiles_per_group, d),
            _resident((1, d)), _resident_layer(w, e), _resident(cw.shape), hspec, hspec,
        ],
        out_specs=[row(dc), st_spec],
        out_shape=[jax.ShapeDtypeStruct((r, dc), BF16), st_shape],
        scratch_shapes=[pltpu.VMEM((SUBLANES + tm, dc), F32)],
        compiler_params=_cparams(1),
        name="in_c",
    )(x, mod[0], mod[0], g, w, cw, h0, h1)


def _out_kernel(*refs, n_in):
    a_refs, w_refs = refs[:n_in], refs[n_in:2 * n_in]
    x_ref, gate_ref, sh_ref, sc_ref, g_ref, o_ref, h_ref = refs[2 * n_in:]
    y = _dot(a_refs[0][...], w_refs[0][...])
    for a_ref, w_ref in zip(a_refs[1:], w_refs[1:]):
        y = y + _dot(a_ref[...], w_ref[...])
    xn = x_ref[...] + gate_ref[...] * y
    o_ref[...] = xn
    h_ref[...] = _modulate(xn, g_ref[...], sh_ref[...], sc_ref[...]).astype(h_ref.dtype)


def _out_proj(acts, weights, e, x, mod, tiles_per_group, tm, g_ffn):
    r, d = x.shape
    row = lambda w: pl.BlockSpec((tm, w), lambda i: (i, 0))
    return pl.pallas_call(
        functools.partial(_out_kernel, n_in=len(acts)),
        grid=(r // tm,),
        in_specs=[row(a.shape[1]) for a in acts] + [_resident_layer(w, e) for w in weights]
        + [row(d), _mod_spec(mod, G1, tiles_per_group, d), _mod_spec(mod, SH2, tiles_per_group, d),
           _mod_spec(mod, SC2, tiles_per_group, d), _resident((1, d))],
        out_specs=[row(d), row(d)],
        out_shape=[jax.ShapeDtypeStruct((r, d), F32), jax.ShapeDtypeStruct((r, d), BF16)],
        compiler_params=_cparams(1),
        name="out_proj",
    )(*acts, *weights, x, mod[0], mod[0], mod[0], g_ffn)


def _ffn_kernel(h_ref, x_hbm, gate_ref, wg_ref, wu_ref, wd_ref, fg_ref, o_ref, x_buf, x_sem, *, final, tm):
    i, j = pl.program_id(0), pl.program_id(1)

    def x_copy():
        return pltpu.make_async_copy(x_hbm.at[pl.ds(pl.multiple_of(i * tm, tm), tm), :], x_buf, x_sem)

    @pl.when(j == 0)
    def _():
        x_copy().start()
        o_ref[...] = jnp.zeros(o_ref.shape, F32)

    h = h_ref[...]
    act = (_silu(_dot(h, wg_ref[...])) * _dot(h, wu_ref[...])).astype(BF16)
    o_ref[...] += _dot(act, wd_ref[...])

    @pl.when(j == pl.num_programs(1) - 1)
    def _():
        x_copy().wait()
        xn = x_buf[...] + gate_ref[...] * o_ref[...]
        if final:
            xn = _rms(xn) * fg_ref[...]
        o_ref[...] = xn


def _ffn(h, x, mod, tiles_per_group, tm, tf, wg, wu, wd, layer, final_g, final):
    r, d = x.shape
    f = wg.shape[2]
    row = pl.BlockSpec((tm, d), lambda i, j: (i, 0))
    return pl.pallas_call(
        functools.partial(_ffn_kernel, final=final, tm=tm),
        grid=(r // tm, f // tf),
        in_specs=[
            row, pl.BlockSpec(memory_space=pl.ANY),
            _mod_spec(mod, G2, tiles_per_group, d),
            pl.BlockSpec((None, d, tf), lambda i, j: (layer, 0, j)),
            pl.BlockSpec((None, d, tf), lambda i, j: (layer, 0, j)),
            pl.BlockSpec((None, tf, d), lambda i, j: (layer, j, 0)),
            _resident((1, d)),
        ],
        out_specs=row,
        out_shape=jax.ShapeDtypeStruct((r, d), F32),
        scratch_shapes=[pltpu.VMEM((tm, d), F32), pltpu.SemaphoreType.DMA(())],
        compiler_params=_cparams(2),
        name="ffn",
    )(h, x, mod[0], wg, wu, wd, final_g)


def _row_tile(t, cap):
    tm = min(t, cap)
    while t % tm:
        tm -= SUBLANES
    return tm


def _ff_tile(f, cap=512):
    tf = min(f, cap)
    while f % tf or tf % LANES:
        tf -= LANES
    return tf


def _trunk(x3, mods, p, cache):
    b, t, d = x3.shape
    depth = mods.shape[0]
    n_even = (depth + 1) // 2
    ha, dh, da, db = p["ha"], p["dh"], p["da"], p["db"]
    kb, kc = p["dw_b_w"].shape[1], p["conv_c_w"].shape[1]
    r = b * t
    x = x3.reshape(r, d)
    if cache is None:
        tm = _row_tile(t, 512)
        tiles_per_group = t // tm
        tiles_per_seq, seg = t // tm, None
        tm_ffn = _row_tile(t, 1024)
    else:
        tm = r
        tiles_per_group = 1
        tiles_per_seq, seg = 1, t
        tm_ffn = r
    kv = None
    fs, bs, cs = [], [], []
    for l in range(depth):
        mod = (mods, l, 1 if cache is None else r)
        e = l // 2
        g_ffn = p["norm_ffn_g"][l]
        if l % 2 == 0:
            q, k_all, v_all, u, lf_t = _in_ab(x, mod, tiles_per_group, tm, p["norm_mix_g"][l], p["wqkv"],
                                              p["wf"], p["bf"], p["wg_ab"], e, n_even, kv, dh)
            kv = (k_all, v_all)
            u3 = u.reshape(b, t, db)
            if cache is None:
                cum = _cumsum(lf_t, b, t)
                att = _attn_prompt(q, k_all, v_all, e, cum, b, t, ha, dh, _row_tile(t, 512))
                hist = jnp.zeros((b, kb - 1, db), F32)
            else:
                past = cache["k"].shape[2]
                npad = LANES
                lf_new = lf_t.reshape(ha, b, t).transpose(1, 0, 2)
                lf_all = jnp.concatenate([cache["logf"][e].transpose(0, 2, 1), lf_new,
                                          jnp.zeros((b, ha, npad - t), F32)], axis=-1)
                cum = _cumsum(lf_all.transpose(1, 0, 2).reshape(ha, b * (past + npad)), b, past + npad)
                pad = lambda a: jnp.pad(a[e].reshape(b, t, da), ((0, 0), (0, npad - t), (0, 0)))
                att = _attn_sample(q.reshape(b, t, da), pad(k_all), pad(v_all), cache["k"], cache["v"], e,
                                   cum).reshape(r, da)
                hist = cache["convb"][e]
            hist_pad = jnp.pad(hist, ((0, 0), (CONV_HALO - (kb - 1), 0), (0, 0)))
            z = _convb(u3, hist_pad, p["dw_b_w"][e], p["dw_b_bias"][e], p["ln_b_g"][e], p["ln_b_b"][e],
                       _row_tile(t, 256)).reshape(r, db)
            x, h = _out_proj([att, z], [p["wout_a"], p["wout_b"]], e, x, mod, tiles_per_group, tm, g_ffn)
            fs.append(lf_t.reshape(ha, b, t).transpose(1, 2, 0))
            keep = kb - 1
            bs.append(u3[:, t - keep:] if t >= keep else jnp.concatenate([hist[:, t:], u3], axis=1))
        else:
            hist = jnp.zeros((b, kc - 1, d), F32) if cache is None else cache["convc"][e]
            if seg is None:
                h0, h1 = hist[:, 0:1], hist[:, 1:2]
            else:
                h0, h1 = jnp.repeat(hist[:, 0], t, axis=0), jnp.repeat(hist[:, 1], t, axis=0)
            gated, st = _in_c(x, mod, tiles_per_group, tm, p["norm_mix_g"][l], p["w_in_c"], e,
                              p["conv_c_w"][e], h0, h1, tiles_per_seq, seg)
            x, h = _out_proj([gated], [p["w_out_c"]], e, x, mod, tiles_per_group, tm, g_ffn)
            cs.append(st[:, -(kc - 1):] if seg is None else st.reshape(b, t, d)[:, -(kc - 1):])
        x = _ffn(h, x, mod, tiles_per_group * tm // tm_ffn, tm_ffn, _ff_tile(p["w_gate"].shape[2]),
                 p["w_gate"], p["w_up"], p["w_down"], l, p["final_g"], final=(l == depth - 1))
    k_all, v_all = kv
    return (x.reshape(b, t, d), k_all.reshape(n_even, b, t, ha, dh), v_all.reshape(n_even, b, t, ha, dh),
            jnp.stack(fs), jnp.stack(bs), jnp.stack(cs))


def kernel(x_prompt, x_sample, cache_k, cache_v, cache_logf, state_convb, state_convc, c_prompt, c_sample,
           ada_w, ada_b, norm_mix_g, norm_ffn_g, w_in_ab, b_f, dw_b_w, dw_b_bias, ln_b_g, ln_b_b, w_out_ab,
           w_in_c, conv_c_w, w_out_c, w_gate, w_up, w_down, final_g):
    bp, tp, d = x_prompt.shape
    bs_, ts, _ = x_sample.shape
    depth = ada_w.shape[0]
    ha = b_f.shape[1]
    d_in = w_in_ab.shape[2]
    da = d_in - ha - 2 * w_out_ab.shape[1]
    db = w_out_ab.shape[1] - da
    dh = da // ha
    assert conv_c_w.shape[1] == 3 and tp % SUBLANES == 0 and (bs_ * ts) % SUBLANES == 0

    bf = lambda a: a.astype(BF16)
    p = {
        "ha": ha, "dh": dh, "da": da, "db": db,
        "norm_mix_g": norm_mix_g.reshape(depth, 1, d), "norm_ffn_g": norm_ffn_g.reshape(depth, 1, d),
        "wqkv": bf(w_in_ab[:, :, 0:3 * da]),
        "wf": bf(w_in_ab[:, :, 3 * da:3 * da + ha].transpose(0, 2, 1)),
        "bf": b_f.reshape(-1, ha, 1),
        "wg_ab": bf(w_in_ab[:, :, 3 * da + ha:]),
        "dw_b_w": dw_b_w, "dw_b_bias": dw_b_bias.reshape(-1, 1, db),
        "ln_b_g": ln_b_g.reshape(-1, 1, db), "ln_b_b": ln_b_b.reshape(-1, 1, db),
        "wout_a": bf(w_out_ab[:, 0:da]), "wout_b": bf(w_out_ab[:, da:]),
        "w_in_c": bf(w_in_c), "conv_c_w": conv_c_w, "w_out_c": bf(w_out_c),
        "w_gate": bf(w_gate), "w_up": bf(w_up), "w_down": bf(w_down),
        "final_g": final_g.reshape(1, d),
    }

    rs = bs_ * ts
    nc = rs + bp
    nc_pad = -(-nc // SUBLANES) * SUBLANES
    c_all = jnp.pad(jnp.concatenate([jnp.repeat(c_sample, ts, axis=0), c_prompt], axis=0),
                    ((0, nc_pad - nc), (0, 0)))
    mod_all = _ada(c_all, ada_w, ada_b)
    mod_s = mod_all.reshape(depth, 1, nc_pad, 6 * d)
    mod_p = mod_all[:, rs:nc].reshape(depth, bp, 1, 6 * d)

    cache = {"k": cache_k, "v": cache_v, "logf": cache_logf, "convb": state_convb, "convc": state_convc}
    y_p, k_p, v_p, f_p, b_p, c_p = _trunk(x_prompt, mod_p, p, None)
    y_s, k_s, v_s, f_s, b_s, c_s = _trunk(x_sample, mod_s, p, cache)
    return (y_p, y_s, k_p, v_p, f_p, b_p, c_p, k_s, v_s, f_s, b_s, c_s)
```

```python
import functools
import math

import jax
import jax.numpy as jnp
from jax import lax
from jax.experimental import pallas as pl
from jax.experimental.pallas import tpu as pltpu

RMS_EPS = 1e-6
LN_EPS = 1e-5
LOG2E = math.log2(math.e)

V7X_VMEM_BYTES = 64 * 1024 * 1024
VMEM_LIMIT = V7X_VMEM_BYTES - 2 * 1024 * 1024
LANES = 128
SUBLANES = 8
CONV_HALO = 32

F32 = jnp.float32
BF16 = jnp.bfloat16

SH1, SC1, G1, SH2, SC2, G2 = range(6)


def _cparams(n_axes):
    return pltpu.CompilerParams(dimension_semantics=("arbitrary",) * n_axes, vmem_limit_bytes=VMEM_LIMIT)


def _resident(shape):
    nd = len(shape)
    return pl.BlockSpec(shape, lambda *_: (0,) * nd, pipeline_mode=pl.Buffered(1))


def _resident_layer(stack, layer):
    nd = stack.ndim - 1
    return pl.BlockSpec((None,) + stack.shape[1:], lambda *_: (layer,) + (0,) * nd, pipeline_mode=pl.Buffered(1))


def _mod_spec(mod, col, tiles_per_group, d):
    _, layer, mr = mod
    return pl.BlockSpec((None, None, mr, d), lambda i, *_: (layer, i // tiles_per_group, 0, col))


def _sigmoid(x):
    return 1.0 / (1.0 + jnp.exp(-x))


def _silu(x):
    return x * _sigmoid(x)


def _rms(x):
    return x * lax.rsqrt(jnp.mean(x * x, axis=-1, keepdims=True) + RMS_EPS)


def _modulate(x, g, shift, scale):
    return (_rms(x) * g) * (1.0 + scale) + shift


def _dot(a, b):
    return jnp.dot(a, b, preferred_element_type=F32)


def _dot_nt(a, b):
    return lax.dot_general(a, b, (((1,), (1,)), ((), ())), preferred_element_type=F32)


def _ada_kernel(c_ref, w_ref, b_ref, o_ref):
    a = _silu(c_ref[...]).astype(BF16)
    o_ref[...] = _dot(a, w_ref[...].astype(BF16)) + b_ref[...]


def _ada(c_all, ada_w, ada_b):
    depth, d, n = ada_w.shape
    r = c_all.shape[0]
    tn = 1024 if n % 1024 == 0 else n
    return pl.pallas_call(
        _ada_kernel,
        grid=(depth, n // tn),
        in_specs=[
            pl.BlockSpec((r, d), lambda l, j: (0, 0)),
            pl.BlockSpec((None, d, tn), lambda l, j: (l, 0, j)),
            pl.BlockSpec((None, 1, tn), lambda l, j: (l, 0, j)),
        ],
        out_specs=pl.BlockSpec((None, r, tn), lambda l, j: (l, 0, j)),
        out_shape=jax.ShapeDtypeStruct((depth, r, n), F32),
        compiler_params=_cparams(2),
        name="ada",
    )(c_all, ada_w, ada_b.reshape(depth, 1, n))


def _in_ab_kernel(*refs, da, db, qscale, n_prev, e):
    x_ref, sh_ref, sc_ref, g_ref, wqkv_ref, wf_ref, bf_ref, wg_ref = refs[:8]
    q_ref, k_ref, v_ref, u_ref, lf_ref = refs[8 + n_prev:]
    h = _modulate(x_ref[...], g_ref[...], sh_ref[...], sc_ref[...]).astype(BF16)
    q_ref[...] = (_dot(h, wqkv_ref[:, 0:da]) * qscale).astype(BF16)
    if n_prev:
        k_ref[...] = _dot(h, wqkv_ref[:, da:2 * da])
        v_ref[...] = _dot(h, wqkv_ref[:, 2 * da:3 * da])
    else:
        for o_ref, c0 in ((k_ref, da), (v_ref, 2 * da)):
            for e2 in range(o_ref.shape[0]):
                if e2 == e:
                    o_ref[e2] = _dot(h, wqkv_ref[:, c0:c0 + da])
                else:
                    o_ref[e2] = jnp.zeros(o_ref.shape[1:], F32)
    cb = 512 if db % 512 == 0 else db
    for c0 in range(0, db, cb):
        ga = _dot(h, wg_ref[:, c0:c0 + cb])
        gb = _dot(h, wg_ref[:, db + c0:db + c0 + cb])
        u_ref[:, c0:c0 + cb] = ga * _sigmoid(gb)
    fz = _dot_nt(wf_ref[...], h) + bf_ref[...]
    lf_ref[...] = jnp.minimum(fz, 0.0) - jnp.log1p(jnp.exp(-jnp.abs(fz)))


def _in_ab(x, mod, tiles_per_group, tm, g, wqkv, wf, bf, wg, e, n_even, kv_prev, dh):
    r, d = x.shape
    da = wqkv.shape[2] // 3
    db = wg.shape[2] // 2
    ha = wf.shape[1]
    row = lambda w: pl.BlockSpec((tm, w), lambda i: (i, 0))
    kv_shape = jax.ShapeDtypeStruct((n_even, r, da), F32)
    n_prev = 0 if kv_prev is None else 2
    if n_prev:
        kv_spec = pl.BlockSpec((None, tm, da), lambda i: (e, i, 0))
    else:
        kv_spec = pl.BlockSpec((n_even, tm, da), lambda i: (0, i, 0))
    outs = pl.pallas_call(
        functools.partial(_in_ab_kernel, da=da, db=db, qscale=LOG2E * float(dh) ** -0.5, n_prev=n_prev, e=e),
        grid=(r // tm,),
        in_specs=[
            row(d),
            _mod_spec(mod, SH1, tiles_per_group, d),
            _mod_spec(mod, SC1, tiles_per_group, d),
            _resident((1, d)),
            _resident_layer(wqkv, e), _resident_layer(wf, e), _resident_layer(bf, e), _resident_layer(wg, e),
        ] + [pl.BlockSpec(memory_space=pl.ANY)] * n_prev,
        out_specs=[row(da), kv_spec, kv_spec, row(db), pl.BlockSpec((ha, tm), lambda i: (0, i))],
        out_shape=[
            jax.ShapeDtypeStruct((r, da), BF16),
            kv_shape, kv_shape,
            jax.ShapeDtypeStruct((r, db), F32),
            jax.ShapeDtypeStruct((ha, r), F32),
        ],
        input_output_aliases={8: 1, 9: 2} if n_prev else {},
        compiler_params=_cparams(1),
        name="in_ab",
    )(x, mod[0], mod[0], g, wqkv, wf, bf, wg, *(kv_prev or ()))
    return outs


def _cumsum_kernel(x_ref, o_ref):
    x = x_ref[...]
    t = x.shape[-1]
    lane = lax.broadcasted_iota(jnp.int32, x.shape, 1)
    s = 1
    while s < t:
        x = x + jnp.where(lane >= s, pltpu.roll(x, s, axis=1), 0.0)
        s *= 2
    o_ref[...] = x


def _cumsum(logf_t, nb, t):
    ha = logf_t.shape[0]
    return pl.pallas_call(
        _cumsum_kernel,
        grid=(nb,),
        in_specs=[pl.BlockSpec((ha, t), lambda b: (0, b))],
        out_specs=pl.BlockSpec((None, ha, t), lambda b: (b, 0, 0)),
        out_shape=jax.ShapeDtypeStruct((nb, ha, t), F32),
        compiler_params=_cparams(1),
        name="cumsum",
    )(logf_t)


def _attn_prompt_kernel(q_ref, k_ref, v_ref, l_ref, o_ref, kt_scr, va_scr, s_scr, p_scr, *, blk, nblk, dh):
    t = nblk * blk
    nl = blk // LANES
    kt_scr[...] = k_ref[...].T.astype(BF16)
    va_scr[:, 0:dh] = v_ref[...].astype(BF16)
    va_scr[:, dh:2 * dh] = jnp.ones((t, dh), BF16)
    row = lax.broadcasted_iota(jnp.int32, (blk, blk), 0)
    col = lax.broadcasted_iota(jnp.int32, (blk, blk), 1)

    def logits_chunk(i, c, mx):
        cs = slice(c * blk, (c + 1) * blk)
        s = _dot(q_ref[i * blk:(i + 1) * blk, :], kt_scr[:, cs]) - l_ref[:, cs] * LOG2E
        if c == i:
            s = jnp.where(col <= row, s, -jnp.inf)
        s_scr[i % 2, :, cs] = s
        for l in range(nl):
            sl = s[:, l * LANES:(l + 1) * LANES]
            mx = sl if mx is None else jnp.maximum(mx, sl)
        return mx

    def exp_chunk(i, c, mb):
        for l in range(c * nl, (c + 1) * nl):
            ls = slice(l * LANES, (l + 1) * LANES)
            p_scr[i % 2, :, ls] = jnp.exp2(s_scr[i % 2, :, ls] - mb).astype(BF16)

    def row_max(mx):
        return jnp.broadcast_to(jnp.max(mx, axis=-1, keepdims=True), (blk, LANES))

    mb = row_max(logits_chunk(0, 0, None))
    for i in range(nblk):
        mx = None
        for c in range(i + 2):
            if i + 1 < nblk:
                mx = logits_chunk(i + 1, c, mx)
            if c <= i:
                exp_chunk(i, c, mb)
        vis = (i + 1) * blk
        acc = _dot(p_scr[i % 2, :, 0:vis], va_scr[0:vis, :])
        o_ref[i * blk:(i + 1) * blk, :] = (acc[:, 0:dh] / acc[:, dh:2 * dh]).astype(o_ref.dtype)
        if i + 1 < nblk:
            mb = row_max(mx)


def _attn_prompt(q, k, v, e, cum, b, t, ha, dh, blk):
    assert dh == LANES and blk % LANES == 0 and t % blk == 0
    nblk = t // blk
    head = pl.BlockSpec((t, dh), lambda bi, h: (bi, h))
    kv_head = pl.BlockSpec((None, t, dh), lambda bi, h: (e, bi, h))
    return pl.pallas_call(
        functools.partial(_attn_prompt_kernel, blk=blk, nblk=nblk, dh=dh),
        grid=(b, ha),
        in_specs=[head, kv_head, kv_head,
                  pl.BlockSpec((None, None, 1, t), lambda bi, h: (bi, h, 0, 0))],
        out_specs=head,
        out_shape=jax.ShapeDtypeStruct((b * t, ha * dh), BF16),
        scratch_shapes=[
            pltpu.VMEM((dh, t), BF16),
            pltpu.VMEM((t, 2 * dh), BF16),
            pltpu.VMEM((2, blk, t), F32),
            pltpu.VMEM((2, blk, t), BF16),
        ],
        compiler_params=_cparams(2),
        name="attn_prompt",
    )(q, k, v, cum.reshape(b, ha, 1, t))


def _attn_sample_kernel(q_ref, kn_ref, vn_ref, ck_ref, cv_ref, lc_ref, ln_ref, o_ref, *, ha, dh, past, npad):
    s_new = q_ref.shape[0]
    n = ha * s_new
    heads = [slice(h * dh, (h + 1) * dh) for h in range(ha)]
    qs = jnp.concatenate([q_ref[:, hs] for hs in heads], axis=0)
    ck = ck_ref[...].reshape(past * ha, dh).astype(BF16)
    cv = cv_ref[...].reshape(past * ha, dh).astype(BF16)
    row_head = lax.broadcasted_iota(jnp.int32, (ha, s_new, past * ha), 0).reshape(n, past * ha)
    col_head = lax.rem(lax.broadcasted_iota(jnp.int32, (n, past * ha), 1), ha)
    sc = jnp.where(col_head == row_head, _dot_nt(qs, ck) - lc_ref[...] * LOG2E, -jnp.inf)
    ln = jnp.concatenate([jnp.broadcast_to(ln_ref[h:h + 1, :], (s_new, npad)) for h in range(ha)], axis=0)
    sn = jnp.concatenate([_dot_nt(q_ref[:, hs], kn_ref[:, hs].astype(BF16)) for hs in heads], axis=0)
    pos = lax.broadcasted_iota(jnp.int32, (ha, s_new, npad), 1).reshape(n, npad)
    col = lax.broadcasted_iota(jnp.int32, (n, npad), 1)
    sn = jnp.where(col <= pos, sn - ln * LOG2E, -jnp.inf)
    m = jnp.maximum(jnp.max(sc, axis=-1, keepdims=True), jnp.max(sn, axis=-1, keepdims=True))
    pc = jnp.exp2(sc - m)
    pn = jnp.exp2(sn - m)
    denom = jnp.sum(pc, axis=-1, keepdims=True) + jnp.sum(pn, axis=-1, keepdims=True)
    acc = _dot(pc.astype(BF16), cv)
    pn = pn.astype(BF16)
    for h, hs in enumerate(heads):
        rs = slice(h * s_new, (h + 1) * s_new)
        o = acc[rs] + _dot(pn[rs], vn_ref[:, hs].astype(BF16))
        o_ref[:, hs] = (o / denom[rs]).astype(o_ref.dtype)


def _attn_sample(q, kn, vn, cache_k, cache_v, layer, cum):
    b, s, da = q.shape
    npad = kn.shape[1]
    _, _, past, ha, dh = cache_k.shape
    assert s % SUBLANES == 0
    cum_cache = cum[:, :, 0:past].transpose(0, 2, 1).reshape(b, 1, past * ha)
    cum_new = cum[:, :, past:past + npad]
    per_b = lambda rows: pl.BlockSpec((None, rows, da), lambda bi: (bi, 0, 0))
    cache = pl.BlockSpec((None, None, past, ha, dh), lambda bi: (layer, bi, 0, 0, 0))
    return pl.pallas_call(
        functools.partial(_attn_sample_kernel, ha=ha, dh=dh, past=past, npad=npad),
        grid=(b,),
        in_specs=[per_b(s), per_b(npad), per_b(npad), cache, cache,
                  pl.BlockSpec((None, 1, past * ha), lambda bi: (bi, 0, 0)),
                  pl.BlockSpec((None, ha, npad), lambda bi: (bi, 0, 0))],
        out_specs=per_b(s),
        out_shape=jax.ShapeDtypeStruct((b, s, da), BF16),
        compiler_params=_cparams(1),
        name="attn_sample",
    )(q, kn, vn, cache_k, cache_v, cum_cache, cum_new)


def _conv_module(u_ref, hist_ref, w_ref, b_ref, g_ref, beta_ref, ext_scr, zs_scr, y_scr, *, kb, tm, rc,
                 tiles_per_seq):
    i = pl.program_id(1)
    c = u_ref.shape[-1]

    @pl.when(i == 0)
    def _():
        ext_scr[0:CONV_HALO, :] = hist_ref[...]

    ext_scr[CONV_HALO:CONV_HALO + tm, :] = u_ref[...]
    off = CONV_HALO - (kb - 1)
    for r0 in range(0, tm, rc):
        for c0 in range(0, c, LANES):
            cl = slice(c0, c0 + LANES)
            y = None
            for s in range(SUBLANES):
                rows = rc if s == 0 else rc + SUBLANES
                zs = None
                for j in range(off, off + kb):
                    if j % SUBLANES != s:
                        continue
                    term = w_ref[j - off:j - off + 1, cl] * ext_scr[r0 + j - s:r0 + j - s + rows, cl]
                    zs = term if zs is None else zs + term
                if zs is None:
                    continue
                if s == 0:
                    shifted = zs
                else:
                    zs_scr[s, :, :] = zs
                    shifted = zs_scr[s, s:s + rc, :]
                y = shifted if y is None else y + shifted
            y_scr[r0:r0 + rc, cl] = y
    y = y_scr[...] + b_ref[...]
    mu = jnp.mean(y, axis=-1, keepdims=True)
    yc = y - mu
    var = jnp.mean(yc * yc, axis=-1, keepdims=True)
    z = yc * lax.rsqrt(var + LN_EPS) * g_ref[...] + beta_ref[...]
    if tiles_per_seq > 1:
        ext_scr[0:CONV_HALO, :] = ext_scr[tm:tm + CONV_HALO, :]
    return _silu(z).astype(BF16)


def _convb_kernel(u_ref, hist_ref, w_ref, b_ref, g_ref, beta_ref, z_ref, ext_scr, zs_scr, y_scr, **conv):
    z_ref[...] = _conv_module(u_ref, hist_ref, w_ref, b_ref, g_ref, beta_ref, ext_scr, zs_scr, y_scr, **conv)


def _convb(u, hist_pad, w, bias, g, beta, tm):
    b, t, c = u.shape
    kb = w.shape[0]
    assert kb - 1 <= CONV_HALO and t % tm == 0 and (tm >= CONV_HALO or t == tm)
    rc = min(tm, 64)
    return pl.pallas_call(
        functools.partial(_convb_kernel, kb=kb, tm=tm, rc=rc, tiles_per_seq=t // tm),
        grid=(b, t // tm),
        in_specs=[
            pl.BlockSpec((None, tm, c), lambda bi, i: (bi, i, 0)),
            pl.BlockSpec((None, CONV_HALO, c), lambda bi, i: (bi, 0, 0)),
            _resident(w.shape), _resident((1, c)), _resident((1, c)), _resident((1, c)),
        ],
        out_specs=pl.BlockSpec((None, tm, c), lambda bi, i: (bi, i, 0)),
        out_shape=jax.ShapeDtypeStruct((b, t, c), BF16),
        scratch_shapes=[pltpu.VMEM((CONV_HALO + tm, c), F32),
                        pltpu.VMEM((SUBLANES, rc + SUBLANES, LANES), F32),
                        pltpu.VMEM((tm, c), F32)],
        compiler_params=_cparams(2),
        name="convb",
    )(u, hist_pad, w, bias, g, beta)


def _mix_ab_kernel(att_ref, u_ref, hist_ref, cw_ref, cb_ref, lg_ref, lb_ref, wa_ref, wb_ref, x_ref, gate_ref,
                   sh_ref, sc_ref, g_ref, o_ref, h_ref, ext_scr, zs_scr, y_scr, **conv):
    ya = _dot(att_ref[...], wa_ref[...])
    z = _conv_module(u_ref, hist_ref, cw_ref, cb_ref, lg_ref, lb_ref, ext_scr, zs_scr, y_scr, **conv)
    xn = x_ref[...] + gate_ref[...] * (ya + _dot(z, wb_ref[...]))
    o_ref[...] = xn
    h_ref[...] = _modulate(xn, g_ref[...], sh_ref[...], sc_ref[...]).astype(h_ref.dtype)


def _mix_ab(att, u, hist_pad, cw, cbias, lg, lb, wa, wb, e, x, mod, tm, g_ffn):
    b, t, db = u.shape
    r, d = x.shape
    kb = cw.shape[0]
    tiles = t // tm
    assert kb - 1 <= CONV_HALO and t % tm == 0 and (tm >= CONV_HALO or t == tm) and mod[2] == 1
    rc = min(tm, 64)
    row = lambda w: pl.BlockSpec((tm, w), lambda bi, i: (bi * tiles + i, 0))
    mrow = lambda col: pl.BlockSpec((None, None, 1, d), lambda bi, i: (mod[1], bi, 0, col))
    return pl.pallas_call(
        functools.partial(_mix_ab_kernel, kb=kb, tm=tm, rc=rc, tiles_per_seq=tiles),
        grid=(b, tiles),
        in_specs=[
            row(att.shape[1]),
            pl.BlockSpec((None, tm, db), lambda bi, i: (bi, i, 0)),
            pl.BlockSpec((None, CONV_HALO, db), lambda bi, i: (bi, 0, 0)),
            _resident(cw.shape), _resident((1, db)), _resident((1, db)), _resident((1, db)),
            _resident_layer(wa, e), _resident_layer(wb, e),
            row(d), mrow(G1), mrow(SH2), mrow(SC2), _resident((1, d)),
        ],
        out_specs=[row(d), row(d)],
        out_shape=[jax.ShapeDtypeStruct((r, d), F32), jax.ShapeDtypeStruct((r, d), BF16)],
        scratch_shapes=[pltpu.VMEM((CONV_HALO + tm, db), F32),
                        pltpu.VMEM((SUBLANES, rc + SUBLANES, LANES), F32),
                        pltpu.VMEM((tm, db), F32)],
        compiler_params=_cparams(2),
        name="mix_ab",
    )(att, u, hist_pad, cw, cbias, lg, lb, wa, wb, x, mod[0], mod[0], mod[0], g_ffn)


def _in_c_kernel(x_ref, sh_ref, sc_ref, g_ref, w_ref, cw_ref, h0_ref, h1_ref, o_ref, st_ref, ext_scr,
                 *, dc, tm, tiles_per_seq, seg, cb):
    h = _modulate(x_ref[...], g_ref[...], sh_ref[...], sc_ref[...]).astype(BF16)
    if seg is None:
        i = pl.program_id(0) % tiles_per_seq

        @pl.when(i == 0)
        def _():
            ext_scr[SUBLANES - 2:SUBLANES - 1, :] = h0_ref[...]
            ext_scr[SUBLANES - 1:SUBLANES, :] = h1_ref[...]
    else:
        ext_scr[0:SUBLANES, :] = jnp.zeros((SUBLANES, dc), F32)
        pos = lax.broadcasted_iota(jnp.int32, (tm, 1), 0) % seg

    for c0 in range(0, dc, cb):
        cl = slice(c0, c0 + cb)
        cx = _dot(h, w_ref[:, dc + c0:dc + c0 + cb]) * _dot(h, w_ref[:, 2 * dc + c0:2 * dc + c0 + cb])
        ext_scr[SUBLANES:SUBLANES + tm, cl] = cx
        m1 = ext_scr[SUBLANES - 1:SUBLANES - 1 + tm, cl]
        m2 = ext_scr[SUBLANES - 2:SUBLANES - 2 + tm, cl]
        if seg is not None:
            m1 = jnp.where(pos == 0, h1_ref[:, cl], m1)
            m2 = jnp.where(pos == 0, h0_ref[:, cl], jnp.where(pos == 1, h1_ref[:, cl], m2))
            st_ref[:, cl] = cx
        uc = cw_ref[0:1, cl] * m2 + cw_ref[1:2, cl] * m1 + cw_ref[2:3, cl] * cx
        o_ref[:, cl] = (_dot(h, w_ref[:, c0:c0 + cb]) * uc).astype(o_ref.dtype)
    if seg is None:
        st_ref[...] = ext_scr[tm:tm + SUBLANES, :]
        if tiles_per_seq > 1:
            ext_scr[0:SUBLANES, :] = ext_scr[tm:tm + SUBLANES, :]


def _in_c(x, mod, tiles_per_group, tm, g, w, e, cw, h0, h1, tiles_per_seq, seg):
    r, d = x.shape
    dc = w.shape[2] // 3
    cb = 512 if dc % 512 == 0 else dc
    row = lambda width: pl.BlockSpec((tm, width), lambda i: (i, 0))
    if seg is None:
        nseq = r // (tm * tiles_per_seq)
        hspec = pl.BlockSpec((None, 1, dc), lambda i: (i // tiles_per_seq, 0, 0))
        st_spec = pl.BlockSpec((None, SUBLANES, dc), lambda i: (i // tiles_per_seq, 0, 0))
        st_shape = jax.ShapeDtypeStruct((nseq, SUBLANES, dc), F32)
    else:
        hspec = row(dc)
        st_spec = row(dc)
        st_shape = jax.ShapeDtypeStruct((r, dc), F32)
    return pl.pallas_call(
        functools.partial(_in_c_kernel, dc=dc, tm=tm, tiles_per_seq=tiles_per_seq, seg=seg, cb=cb),
        grid=(r // tm,),
        in_specs=[
            row(d),
            _mod_spec(mod, SH1, tiles_per_group, d),
            _mod_spec(mod, SC1, tiles_per_group, d),
            _resident((1, d)), _resident_layer(w, e), _resident(cw.shape), hspec, hspec,
        ],
        out_specs=[row(dc), st_spec],
        out_shape=[jax.ShapeDtypeStruct((r, dc), BF16), st_shape],
        scratch_shapes=[pltpu.VMEM((SUBLANES + tm, dc), F32)],
        compiler_params=_cparams(1),
        name="in_c",
    )(x, mod[0], mod[0], g, w, cw, h0, h1)


def _out_kernel(*refs, n_in):
    a_refs, w_refs = refs[:n_in], refs[n_in:2 * n_in]
    x_ref, gate_ref, sh_ref, sc_ref, g_ref, o_ref, h_ref = refs[2 * n_in:]
    y = _dot(a_refs[0][...], w_refs[0][...])
    for a_ref, w_ref in zip(a_refs[1:], w_refs[1:]):
        y = y + _dot(a_ref[...], w_ref[...])
    xn = x_ref[...] + gate_ref[...] * y
    o_ref[...] = xn
    h_ref[...] = _modulate(xn, g_ref[...], sh_ref[...], sc_ref[...]).astype(h_ref.dtype)


def _out_proj(acts, weights, e, x, mod, tiles_per_group, tm, g_ffn):
    r, d = x.shape
    row = lambda w: pl.BlockSpec((tm, w), lambda i: (i, 0))
    return pl.pallas_call(
        functools.partial(_out_kernel, n_in=len(acts)),
        grid=(r // tm,),
        in_specs=[row(a.shape[1]) for a in acts] + [_resident_layer(w, e) for w in weights]
        + [row(d), _mod_spec(mod, G1, tiles_per_group, d), _mod_spec(mod, SH2, tiles_per_group, d),
           _mod_spec(mod, SC2, tiles_per_group, d), _resident((1, d))],
        out_specs=[row(d), row(d)],
        out_shape=[jax.ShapeDtypeStruct((r, d), F32), jax.ShapeDtypeStruct((r, d), BF16)],
        compiler_params=_cparams(1),
        name="out_proj",
    )(*acts, *weights, x, mod[0], mod[0], mod[0], g_ffn)


def _ffn_kernel(h_ref, x_hbm, gate_ref, wg_ref, wu_ref, wd_ref, fg_ref, o_ref, x_buf, x_sem, *, final, tm):
    i, j = pl.program_id(0), pl.program_id(1)

    def x_copy():
        return pltpu.make_async_copy(x_hbm.at[pl.ds(pl.multiple_of(i * tm, tm), tm), :], x_buf, x_sem)

    @pl.when(j == 0)
    def _():
        x_copy().start()
        o_ref[...] = jnp.zeros(o_ref.shape, F32)

    h = h_ref[...]
    act = (_silu(_dot(h, wg_ref[...])) * _dot(h, wu_ref[...])).astype(BF16)
    o_ref[...] += _dot(act, wd_ref[...])

    @pl.when(j == pl.num_programs(1) - 1)
    def _():
        x_copy().wait()
        xn = x_buf[...] + gate_ref[...] * o_ref[...]
        if final:
            xn = _rms(xn) * fg_ref[...]
        o_ref[...] = xn


def _ffn(h, x, mod, tiles_per_group, tm, tf, wg, wu, wd, layer, final_g, final):
    r, d = x.shape
    f = wg.shape[2]
    row = pl.BlockSpec((tm, d), lambda i, j: (i, 0))
    return pl.pallas_call(
        functools.partial(_ffn_kernel, final=final, tm=tm),
        grid=(r // tm, f // tf),
        in_specs=[
            row, pl.BlockSpec(memory_space=pl.ANY),
            _mod_spec(mod, G2, tiles_per_group, d),
            pl.BlockSpec((None, d, tf), lambda i, j: (layer, 0, j)),
            pl.BlockSpec((None, d, tf), lambda i, j: (layer, 0, j)),
            pl.BlockSpec((None, tf, d), lambda i, j: (layer, j, 0)),
            _resident((1, d)),
        ],
        out_specs=row,
        out_shape=jax.ShapeDtypeStruct((r, d), F32),
        scratch_shapes=[pltpu.VMEM((tm, d), F32), pltpu.SemaphoreType.DMA(())],
        compiler_params=_cparams(2),
        name="ffn",
    )(h, x, mod[0], wg, wu, wd, final_g)


def _row_tile(t, cap):
    tm = min(t, cap)
    while t % tm:
        tm -= SUBLANES
    return tm


def _ff_tile(f, cap=512):
    tf = min(f, cap)
    while f % tf or tf % LANES:
        tf -= LANES
    return tf


def _trunk(x3, mods, p, cache):
    b, t, d = x3.shape
    depth = mods.shape[0]
    n_even = (depth + 1) // 2
    ha, dh, da, db = p["ha"], p["dh"], p["da"], p["db"]
    kb, kc = p["dw_b_w"].shape[1], p["conv_c_w"].shape[1]
    r = b * t
    x = x3.reshape(r, d)
    if cache is None:
        tm = _row_tile(t, 512)
        tiles_per_group = t // tm
        tiles_per_seq, seg = t // tm, None
        tm_ffn = _row_tile(t, 1024)
    else:
        tm = r
        tiles_per_group = 1
        tiles_per_seq, seg = 1, t
        tm_ffn = r
    kv = None
    fs, bs, cs = [], [], []
    for l in range(depth):
        mod = (mods, l, 1 if cache is None else r)
        e = l // 2
        g_ffn = p["norm_ffn_g"][l]
        if l % 2 == 0:
            q, k_all, v_all, u, lf_t = _in_ab(x, mod, tiles_per_group, tm, p["norm_mix_g"][l], p["wqkv"],
                                              p["wf"], p["bf"], p["wg_ab"], e, n_even, kv, dh)
            kv = (k_all, v_all)
            u3 = u.reshape(b, t, db)
            if cache is None:
                cum = _cumsum(lf_t, b, t)
                att = _attn_prompt(q, k_all, v_all, e, cum, b, t, ha, dh, _row_tile(t, 256))
                hist = jnp.zeros((b, kb - 1, db), F32)
            else:
                past = cache["k"].shape[2]
                npad = LANES
                lf_new = lf_t.reshape(ha, b, t).transpose(1, 0, 2)
                lf_all = jnp.concatenate([cache["logf"][e].transpose(0, 2, 1), lf_new,
                                          jnp.zeros((b, ha, npad - t), F32)], axis=-1)
                cum = _cumsum(lf_all.transpose(1, 0, 2).reshape(ha, b * (past + npad)), b, past + npad)
                pad = lambda a: jnp.pad(a[e].reshape(b, t, da), ((0, 0), (0, npad - t), (0, 0)))
                att = _attn_sample(q.reshape(b, t, da), pad(k_all), pad(v_all), cache["k"], cache["v"], e,
                                   cum).reshape(r, da)
                hist = cache["convb"][e]
            hist_pad = jnp.pad(hist, ((0, 0), (CONV_HALO - (kb - 1), 0), (0, 0)))
            conv_p = (p["dw_b_w"][e], p["dw_b_bias"][e], p["ln_b_g"][e], p["ln_b_b"][e])
            if cache is None:
                x, h = _mix_ab(att, u3, hist_pad, *conv_p, p["wout_a"], p["wout_b"], e, x, mod, tm, g_ffn)
            else:
                z = _convb(u3, hist_pad, *conv_p, _row_tile(t, 512)).reshape(r, db)
                x, h = _out_proj([att, z], [p["wout_a"], p["wout_b"]], e, x, mod, tiles_per_group, tm, g_ffn)
            fs.append(lf_t.reshape(ha, b, t).transpose(1, 2, 0))
            keep = kb - 1
            bs.append(u3[:, t - keep:] if t >= keep else jnp.concatenate([hist[:, t:], u3], axis=1))
        else:
            hist = jnp.zeros((b, kc - 1, d), F32) if cache is None else cache["convc"][e]
            if seg is None:
                h0, h1 = hist[:, 0:1], hist[:, 1:2]
            else:
                h0, h1 = jnp.repeat(hist[:, 0], t, axis=0), jnp.repeat(hist[:, 1], t, axis=0)
            gated, st = _in_c(x, mod, tiles_per_group, tm, p["norm_mix_g"][l], p["w_in_c"], e,
                              p["conv_c_w"][e], h0, h1, tiles_per_seq, seg)
            x, h = _out_proj([gated], [p["w_out_c"]], e, x, mod, tiles_per_group, tm, g_ffn)
            cs.append(st[:, -(kc - 1):] if seg is None else st.reshape(b, t, d)[:, -(kc - 1):])
        x = _ffn(h, x, mod, tiles_per_group * tm // tm_ffn, tm_ffn, _ff_tile(p["w_gate"].shape[2]),
                 p["w_gate"], p["w_up"], p["w_down"], l, p["final_g"], final=(l == depth - 1))
    k_all, v_all = kv
    return (x.reshape(b, t, d), k_all.reshape(n_even, b, t, ha, dh), v_all.reshape(n_even, b, t, ha, dh),
            jnp.stack(fs), jnp.stack(bs), jnp.stack(cs))


def kernel(x_prompt, x_sample, cache_k, cache_v, cache_logf, state_convb, state_convc, c_prompt, c_sample,
           ada_w, ada_b, norm_mix_g, norm_ffn_g, w_in_ab, b_f, dw_b_w, dw_b_bias, ln_b_g, ln_b_b, w_out_ab,
           w_in_c, conv_c_w, w_out_c, w_gate, w_up, w_down, final_g):
    bp, tp, d = x_prompt.shape
    bs_, ts, _ = x_sample.shape
    depth = ada_w.shape[0]
    ha = b_f.shape[1]
    d_in = w_in_ab.shape[2]
    da = d_in - ha - 2 * w_out_ab.shape[1]
    db = w_out_ab.shape[1] - da
    dh = da // ha
    assert conv_c_w.shape[1] == 3 and tp % SUBLANES == 0 and (bs_ * ts) % SUBLANES == 0

    bf = lambda a: a.astype(BF16)
    p = {
        "ha": ha, "dh": dh, "da": da, "db": db,
        "norm_mix_g": norm_mix_g.reshape(depth, 1, d), "norm_ffn_g": norm_ffn_g.reshape(depth, 1, d),
        "wqkv": bf(w_in_ab[:, :, 0:3 * da]),
        "wf": bf(w_in_ab[:, :, 3 * da:3 * da + ha].transpose(0, 2, 1)),
        "bf": b_f.reshape(-1, ha, 1),
        "wg_ab": bf(w_in_ab[:, :, 3 * da + ha:]),
        "dw_b_w": dw_b_w, "dw_b_bias": dw_b_bias.reshape(-1, 1, db),
        "ln_b_g": ln_b_g.reshape(-1, 1, db), "ln_b_b": ln_b_b.reshape(-1, 1, db),
        "wout_a": bf(w_out_ab[:, 0:da]), "wout_b": bf(w_out_ab[:, da:]),
        "w_in_c": bf(w_in_c), "conv_c_w": conv_c_w, "w_out_c": bf(w_out_c),
        "w_gate": bf(w_gate), "w_up": bf(w_up), "w_down": bf(w_down),
        "final_g": final_g.reshape(1, d),
    }

    rs = bs_ * ts
    nc = rs + bp
    nc_pad = -(-nc // SUBLANES) * SUBLANES
    c_all = jnp.pad(jnp.concatenate([jnp.repeat(c_sample, ts, axis=0), c_prompt], axis=0),
                    ((0, nc_pad - nc), (0, 0)))
    mod_all = _ada(c_all, ada_w, ada_b)
    mod_s = mod_all.reshape(depth, 1, nc_pad, 6 * d)
    mod_p = mod_all[:, rs:nc].reshape(depth, bp, 1, 6 * d)

    cache = {"k": cache_k, "v": cache_v, "logf": cache_logf, "convb": state_convb, "convc": state_convc}
    y_p, k_p, v_p, f_p, b_p, c_p = _trunk(x_prompt, mod_p, p, None)
    y_s, k_s, v_s, f_s, b_s, c_s = _trunk(x_sample, mod_s, p, cache)
    return (y_p, y_s, k_p, v_p, f_p, b_p, c_p, k_s, v_s, f_s, b_s, c_s)
```

```python
import functools
import math

import jax
import jax.numpy as jnp
from jax import lax
from jax.experimental import pallas as pl
from jax.experimental.pallas import tpu as pltpu

RMS_EPS = 1e-6
LN_EPS = 1e-5
LOG2E = math.log2(math.e)

V7X_VMEM_BYTES = 64 * 1024 * 1024
VMEM_LIMIT = V7X_VMEM_BYTES - 2 * 1024 * 1024
LANES = 128
SUBLANES = 8
CONV_HALO = 32

F32 = jnp.float32
BF16 = jnp.bfloat16

SH1, SC1, G1, SH2, SC2, G2 = range(6)


def _cparams(n_axes):
    return pltpu.CompilerParams(dimension_semantics=("arbitrary",) * n_axes, vmem_limit_bytes=VMEM_LIMIT)


def _resident(shape):
    nd = len(shape)
    return pl.BlockSpec(shape, lambda *_: (0,) * nd, pipeline_mode=pl.Buffered(1))


def _resident_layer(stack, layer):
    nd = stack.ndim - 1
    return pl.BlockSpec((None,) + stack.shape[1:], lambda *_: (layer,) + (0,) * nd, pipeline_mode=pl.Buffered(1))


def _mod_spec(mod, col, tiles_per_group, d):
    _, layer, mr = mod
    return pl.BlockSpec((None, None, mr, d), lambda i, *_: (layer, i // tiles_per_group, 0, col))


def _sigmoid(x):
    return 1.0 / (1.0 + jnp.exp(-x))


def _silu(x):
    return x * _sigmoid(x)


def _rms(x):
    return x * lax.rsqrt(jnp.mean(x * x, axis=-1, keepdims=True) + RMS_EPS)


def _modulate(x, g, shift, scale):
    return (_rms(x) * g) * (1.0 + scale) + shift


def _dot(a, b):
    return jnp.dot(a, b, preferred_element_type=F32)


def _dot_nt(a, b):
    return lax.dot_general(a, b, (((1,), (1,)), ((), ())), preferred_element_type=F32)


def _ada_kernel(c_ref, w_ref, b_ref, o_ref):
    a = _silu(c_ref[...]).astype(BF16)
    o_ref[...] = _dot(a, w_ref[...].astype(BF16)) + b_ref[...]


def _ada(c_all, ada_w, ada_b):
    depth, d, n = ada_w.shape
    r = c_all.shape[0]
    tn = 1024 if n % 1024 == 0 else n
    return pl.pallas_call(
        _ada_kernel,
        grid=(depth, n // tn),
        in_specs=[
            pl.BlockSpec((r, d), lambda l, j: (0, 0)),
            pl.BlockSpec((None, d, tn), lambda l, j: (l, 0, j)),
            pl.BlockSpec((None, 1, tn), lambda l, j: (l, 0, j)),
        ],
        out_specs=pl.BlockSpec((None, r, tn), lambda l, j: (l, 0, j)),
        out_shape=jax.ShapeDtypeStruct((depth, r, n), F32),
        compiler_params=_cparams(2),
        name="ada",
    )(c_all, ada_w, ada_b.reshape(depth, 1, n))


def _in_ab_kernel(*refs, da, db, qscale, n_prev, e):
    x_ref, sh_ref, sc_ref, g_ref, wqkv_ref, wf_ref, bf_ref, wg_ref = refs[:8]
    q_ref, k_ref, v_ref, u_ref, lf_ref = refs[8 + n_prev:]
    h = _modulate(x_ref[...], g_ref[...], sh_ref[...], sc_ref[...]).astype(BF16)
    q_ref[...] = (_dot(h, wqkv_ref[:, 0:da]) * qscale).astype(BF16)
    if n_prev:
        k_ref[...] = _dot(h, wqkv_ref[:, da:2 * da])
        v_ref[...] = _dot(h, wqkv_ref[:, 2 * da:3 * da])
    else:
        for o_ref, c0 in ((k_ref, da), (v_ref, 2 * da)):
            for e2 in range(o_ref.shape[0]):
                if e2 == e:
                    o_ref[e2] = _dot(h, wqkv_ref[:, c0:c0 + da])
                else:
                    o_ref[e2] = jnp.zeros(o_ref.shape[1:], F32)
    cb = 512 if db % 512 == 0 else db
    for c0 in range(0, db, cb):
        ga = _dot(h, wg_ref[:, c0:c0 + cb])
        gb = _dot(h, wg_ref[:, db + c0:db + c0 + cb])
        u_ref[:, c0:c0 + cb] = ga * _sigmoid(gb)
    fz = _dot_nt(wf_ref[...], h) + bf_ref[...]
    lf_ref[...] = jnp.minimum(fz, 0.0) - jnp.log1p(jnp.exp(-jnp.abs(fz)))


def _in_ab(x, mod, tiles_per_group, tm, g, wqkv, wf, bf, wg, e, n_even, kv_prev, dh):
    r, d = x.shape
    da = wqkv.shape[2] // 3
    db = wg.shape[2] // 2
    ha = wf.shape[1]
    row = lambda w: pl.BlockSpec((tm, w), lambda i: (i, 0))
    kv_shape = jax.ShapeDtypeStruct((n_even, r, da), F32)
    n_prev = 0 if kv_prev is None else 2
    if n_prev:
        kv_spec = pl.BlockSpec((None, tm, da), lambda i: (e, i, 0))
    else:
        kv_spec = pl.BlockSpec((n_even, tm, da), lambda i: (0, i, 0))
    outs = pl.pallas_call(
        functools.partial(_in_ab_kernel, da=da, db=db, qscale=LOG2E * float(dh) ** -0.5, n_prev=n_prev, e=e),
        grid=(r // tm,),
        in_specs=[
            row(d),
            _mod_spec(mod, SH1, tiles_per_group, d),
            _mod_spec(mod, SC1, tiles_per_group, d),
            _resident((1, d)),
            _resident_layer(wqkv, e), _resident_layer(wf, e), _resident_layer(bf, e), _resident_layer(wg, e),
        ] + [pl.BlockSpec(memory_space=pl.ANY)] * n_prev,
        out_specs=[row(da), kv_spec, kv_spec, row(db), pl.BlockSpec((ha, tm), lambda i: (0, i))],
        out_shape=[
            jax.ShapeDtypeStruct((r, da), BF16),
            kv_shape, kv_shape,
            jax.ShapeDtypeStruct((r, db), F32),
            jax.ShapeDtypeStruct((ha, r), F32),
        ],
        input_output_aliases={8: 1, 9: 2} if n_prev else {},
        compiler_params=_cparams(1),
        name="in_ab",
    )(x, mod[0], mod[0], g, wqkv, wf, bf, wg, *(kv_prev or ()))
    return outs


def _cumsum_kernel(x_ref, o_ref):
    x = x_ref[...]
    t = x.shape[-1]
    lane = lax.broadcasted_iota(jnp.int32, x.shape, 1)
    s = 1
    while s < t:
        x = x + jnp.where(lane >= s, pltpu.roll(x, s, axis=1), 0.0)
        s *= 2
    o_ref[...] = x


def _cumsum(logf_t, nb, t):
    ha = logf_t.shape[0]
    return pl.pallas_call(
        _cumsum_kernel,
        grid=(nb,),
        in_specs=[pl.BlockSpec((ha, t), lambda b: (0, b))],
        out_specs=pl.BlockSpec((None, ha, t), lambda b: (b, 0, 0)),
        out_shape=jax.ShapeDtypeStruct((nb, ha, t), F32),
        compiler_params=_cparams(1),
        name="cumsum",
    )(logf_t)


def _attn_prompt_kernel(q_ref, k_ref, v_ref, l_ref, o_ref, kt_scr, va_scr, s_scr, p_scr, *, blk, nblk, dh):
    t = nblk * blk
    nl = blk // LANES
    kt_scr[...] = k_ref[...].T.astype(BF16)
    va_scr[:, 0:dh] = v_ref[...].astype(BF16)
    va_scr[:, dh:2 * dh] = jnp.ones((t, dh), BF16)
    row = lax.broadcasted_iota(jnp.int32, (blk, blk), 0)
    col = lax.broadcasted_iota(jnp.int32, (blk, blk), 1)

    def logits_chunk(i, c, mx):
        cs = slice(c * blk, (c + 1) * blk)
        s = _dot(q_ref[i * blk:(i + 1) * blk, :], kt_scr[:, cs]) - l_ref[:, cs] * LOG2E
        if c == i:
            s = jnp.where(col <= row, s, -jnp.inf)
        s_scr[i % 2, :, cs] = s
        for l in range(nl):
            sl = s[:, l * LANES:(l + 1) * LANES]
            mx = sl if mx is None else jnp.maximum(mx, sl)
        return mx

    def exp_chunk(i, c, mb):
        for l in range(c * nl, (c + 1) * nl):
            ls = slice(l * LANES, (l + 1) * LANES)
            p_scr[i % 2, :, ls] = jnp.exp2(s_scr[i % 2, :, ls] - mb).astype(BF16)

    def row_max(mx):
        return jnp.broadcast_to(jnp.max(mx, axis=-1, keepdims=True), (blk, LANES))

    mb = row_max(logits_chunk(0, 0, None))
    for i in range(nblk):
        mx = None
        for c in range(i + 2):
            if i + 1 < nblk:
                mx = logits_chunk(i + 1, c, mx)
            if c <= i:
                exp_chunk(i, c, mb)
        vis = (i + 1) * blk
        acc = _dot(p_scr[i % 2, :, 0:vis], va_scr[0:vis, :])
        o_ref[i * blk:(i + 1) * blk, :] = (acc[:, 0:dh] / acc[:, dh:2 * dh]).astype(o_ref.dtype)
        if i + 1 < nblk:
            mb = row_max(mx)


def _attn_prompt(q, k, v, e, cum, b, t, ha, dh, blk):
    assert dh == LANES and blk % LANES == 0 and t % blk == 0
    nblk = t // blk
    head = pl.BlockSpec((t, dh), lambda bi, h: (bi, h))
    kv_head = pl.BlockSpec((None, t, dh), lambda bi, h: (e, bi, h))
    return pl.pallas_call(
        functools.partial(_attn_prompt_kernel, blk=blk, nblk=nblk, dh=dh),
        grid=(b, ha),
        in_specs=[head, kv_head, kv_head,
                  pl.BlockSpec((None, None, 1, t), lambda bi, h: (bi, h, 0, 0))],
        out_specs=head,
        out_shape=jax.ShapeDtypeStruct((b * t, ha * dh), BF16),
        scratch_shapes=[
            pltpu.VMEM((dh, t), BF16),
            pltpu.VMEM((t, 2 * dh), BF16),
            pltpu.VMEM((2, blk, t), F32),
            pltpu.VMEM((2, blk, t), BF16),
        ],
        compiler_params=_cparams(2),
        name="attn_prompt",
    )(q, k, v, cum.reshape(b, ha, 1, t))


def _attn_sample_kernel(q_ref, kn_ref, vn_ref, ck_ref, cv_ref, lc_ref, ln_ref, o_ref, *, ha, dh, past, npad):
    s_new = q_ref.shape[0]
    n = ha * s_new
    heads = [slice(h * dh, (h + 1) * dh) for h in range(ha)]
    qs = jnp.concatenate([q_ref[:, hs] for hs in heads], axis=0)
    ck = ck_ref[...].reshape(past * ha, dh).astype(BF16)
    cv = cv_ref[...].reshape(past * ha, dh).astype(BF16)
    row_head = lax.broadcasted_iota(jnp.int32, (ha, s_new, past * ha), 0).reshape(n, past * ha)
    col_head = lax.rem(lax.broadcasted_iota(jnp.int32, (n, past * ha), 1), ha)
    sc = jnp.where(col_head == row_head, _dot_nt(qs, ck) - lc_ref[...] * LOG2E, -jnp.inf)
    ln = jnp.concatenate([jnp.broadcast_to(ln_ref[h:h + 1, :], (s_new, npad)) for h in range(ha)], axis=0)
    sn = jnp.concatenate([_dot_nt(q_ref[:, hs], kn_ref[:, hs].astype(BF16)) for hs in heads], axis=0)
    pos = lax.broadcasted_iota(jnp.int32, (ha, s_new, npad), 1).reshape(n, npad)
    col = lax.broadcasted_iota(jnp.int32, (n, npad), 1)
    sn = jnp.where(col <= pos, sn - ln * LOG2E, -jnp.inf)
    m = jnp.maximum(jnp.max(sc, axis=-1, keepdims=True), jnp.max(sn, axis=-1, keepdims=True))
    pc = jnp.exp2(sc - m)
    pn = jnp.exp2(sn - m)
    denom = jnp.sum(pc, axis=-1, keepdims=True) + jnp.sum(pn, axis=-1, keepdims=True)
    acc = _dot(pc.astype(BF16), cv)
    pn = pn.astype(BF16)
    for h, hs in enumerate(heads):
        rs = slice(h * s_new, (h + 1) * s_new)
        o = acc[rs] + _dot(pn[rs], vn_ref[:, hs].astype(BF16))
        o_ref[:, hs] = (o / denom[rs]).astype(o_ref.dtype)


def _attn_sample(q, kn, vn, cache_k, cache_v, layer, cum):
    b, s, da = q.shape
    npad = kn.shape[1]
    _, _, past, ha, dh = cache_k.shape
    assert s % SUBLANES == 0
    cum_cache = cum[:, :, 0:past].transpose(0, 2, 1).reshape(b, 1, past * ha)
    cum_new = cum[:, :, past:past + npad]
    per_b = lambda rows: pl.BlockSpec((None, rows, da), lambda bi: (bi, 0, 0))
    cache = pl.BlockSpec((None, None, past, ha, dh), lambda bi: (layer, bi, 0, 0, 0))
    return pl.pallas_call(
        functools.partial(_attn_sample_kernel, ha=ha, dh=dh, past=past, npad=npad),
        grid=(b,),
        in_specs=[per_b(s), per_b(npad), per_b(npad), cache, cache,
                  pl.BlockSpec((None, 1, past * ha), lambda bi: (bi, 0, 0)),
                  pl.BlockSpec((None, ha, npad), lambda bi: (bi, 0, 0))],
        out_specs=per_b(s),
        out_shape=jax.ShapeDtypeStruct((b, s, da), BF16),
        compiler_params=_cparams(1),
        name="attn_sample",
    )(q, kn, vn, cache_k, cache_v, cum_cache, cum_new)


def _conv_module(u_ref, hist_ref, w_ref, b_ref, g_ref, beta_ref, ext_scr, zs_scr, y_scr, *, kb, tm, rc,
                 tiles_per_seq, between=()):
    between = list(between)
    i = pl.program_id(1)
    c = u_ref.shape[-1]

    @pl.when(i == 0)
    def _():
        ext_scr[0:CONV_HALO, :] = hist_ref[...]

    ext_scr[CONV_HALO:CONV_HALO + tm, :] = u_ref[...]
    off = CONV_HALO - (kb - 1)
    for r0 in range(0, tm, rc):
        for c0 in range(0, c, LANES):
            cl = slice(c0, c0 + LANES)
            y = None
            for s in range(SUBLANES):
                rows = rc if s == 0 else rc + SUBLANES
                zs = None
                for j in range(off, off + kb):
                    if j % SUBLANES != s:
                        continue
                    term = w_ref[j - off:j - off + 1, cl] * ext_scr[r0 + j - s:r0 + j - s + rows, cl]
                    zs = term if zs is None else zs + term
                if zs is None:
                    continue
                if s == 0:
                    shifted = zs
                else:
                    zs_scr[s, :, :] = zs
                    shifted = zs_scr[s, s:s + rc, :]
                y = shifted if y is None else y + shifted
            y_scr[r0:r0 + rc, cl] = y
        if between:
            between.pop(0)()
    for piece in between:
        piece()
    y = y_scr[...] + b_ref[...]
    mu = jnp.mean(y, axis=-1, keepdims=True)
    yc = y - mu
    var = jnp.mean(yc * yc, axis=-1, keepdims=True)
    z = yc * lax.rsqrt(var + LN_EPS) * g_ref[...] + beta_ref[...]
    if tiles_per_seq > 1:
        ext_scr[0:CONV_HALO, :] = ext_scr[tm:tm + CONV_HALO, :]
    return _silu(z).astype(BF16)


def _convb_kernel(u_ref, hist_ref, w_ref, b_ref, g_ref, beta_ref, z_ref, ext_scr, zs_scr, y_scr, **conv):
    z_ref[...] = _conv_module(u_ref, hist_ref, w_ref, b_ref, g_ref, beta_ref, ext_scr, zs_scr, y_scr, **conv)


def _convb(u, hist_pad, w, bias, g, beta, tm):
    b, t, c = u.shape
    kb = w.shape[0]
    assert kb - 1 <= CONV_HALO and t % tm == 0 and (tm >= CONV_HALO or t == tm)
    rc = min(tm, 64)
    return pl.pallas_call(
        functools.partial(_convb_kernel, kb=kb, tm=tm, rc=rc, tiles_per_seq=t // tm),
        grid=(b, t // tm),
        in_specs=[
            pl.BlockSpec((None, tm, c), lambda bi, i: (bi, i, 0)),
            pl.BlockSpec((None, CONV_HALO, c), lambda bi, i: (bi, 0, 0)),
            _resident(w.shape), _resident((1, c)), _resident((1, c)), _resident((1, c)),
        ],
        out_specs=pl.BlockSpec((None, tm, c), lambda bi, i: (bi, i, 0)),
        out_shape=jax.ShapeDtypeStruct((b, t, c), BF16),
        scratch_shapes=[pltpu.VMEM((CONV_HALO + tm, c), F32),
                        pltpu.VMEM((SUBLANES, rc + SUBLANES, LANES), F32),
                        pltpu.VMEM((tm, c), F32)],
        compiler_params=_cparams(2),
        name="convb",
    )(u, hist_pad, w, bias, g, beta)


def _mix_ab_kernel(att_ref, u_ref, hist_ref, cw_ref, cb_ref, lg_ref, lb_ref, wa_ref, wb_ref, x_ref, gate_ref,
                   sh_ref, sc_ref, g_ref, o_ref, h_ref, ext_scr, zs_scr, y_scr, ya_scr, *, nb, **conv):
    d = o_ref.shape[-1]
    cw = d // nb

    def att_block(k):
        def piece():
            ya_scr[:, k * cw:(k + 1) * cw] = _dot(att_ref[...], wa_ref[:, k * cw:(k + 1) * cw])
        return piece

    z = _conv_module(u_ref, hist_ref, cw_ref, cb_ref, lg_ref, lb_ref, ext_scr, zs_scr, y_scr,
                     between=[att_block(k) for k in range(nb)], **conv)
    xn = x_ref[...] + gate_ref[...] * (ya_scr[...] + _dot(z, wb_ref[...]))
    o_ref[...] = xn
    h_ref[...] = _modulate(xn, g_ref[...], sh_ref[...], sc_ref[...]).astype(h_ref.dtype)


def _mix_ab(att, u, hist_pad, cw, cbias, lg, lb, wa, wb, e, x, mod, tm, g_ffn):
    b, t, db = u.shape
    r, d = x.shape
    kb = cw.shape[0]
    tiles = t // tm
    assert kb - 1 <= CONV_HALO and t % tm == 0 and (tm >= CONV_HALO or t == tm) and mod[2] == 1
    rc = min(tm, 64)
    nb = tm // rc
    if d % (nb * 2 * LANES):
        nb = 1
    row = lambda w: pl.BlockSpec((tm, w), lambda bi, i: (bi * tiles + i, 0))
    mrow = lambda col: pl.BlockSpec((None, None, 1, d), lambda bi, i: (mod[1], bi, 0, col))
    return pl.pallas_call(
        functools.partial(_mix_ab_kernel, kb=kb, tm=tm, rc=rc, tiles_per_seq=tiles, nb=nb),
        grid=(b, tiles),
        in_specs=[
            row(att.shape[1]),
            pl.BlockSpec((None, tm, db), lambda bi, i: (bi, i, 0)),
            pl.BlockSpec((None, CONV_HALO, db), lambda bi, i: (bi, 0, 0)),
            _resident(cw.shape), _resident((1, db)), _resident((1, db)), _resident((1, db)),
            _resident_layer(wa, e), _resident_layer(wb, e),
            row(d), mrow(G1), mrow(SH2), mrow(SC2), _resident((1, d)),
        ],
        out_specs=[row(d), row(d)],
        out_shape=[jax.ShapeDtypeStruct((r, d), F32), jax.ShapeDtypeStruct((r, d), BF16)],
        scratch_shapes=[pltpu.VMEM((CONV_HALO + tm, db), F32),
                        pltpu.VMEM((SUBLANES, rc + SUBLANES, LANES), F32),
                        pltpu.VMEM((tm, db), F32),
                        pltpu.VMEM((tm, d), F32)],
        compiler_params=_cparams(2),
        name="mix_ab",
    )(att, u, hist_pad, cw, cbias, lg, lb, wa, wb, x, mod[0], mod[0], mod[0], g_ffn)


def _in_c_kernel(x_ref, sh_ref, sc_ref, g_ref, w_ref, cw_ref, h0_ref, h1_ref, o_ref, st_ref, ext_scr,
                 *, dc, tm, tiles_per_seq, seg, cb):
    h = _modulate(x_ref[...], g_ref[...], sh_ref[...], sc_ref[...]).astype(BF16)
    if seg is None:
        i = pl.program_id(0) % tiles_per_seq

        @pl.when(i == 0)
        def _():
            ext_scr[SUBLANES - 2:SUBLANES - 1, :] = h0_ref[...]
            ext_scr[SUBLANES - 1:SUBLANES, :] = h1_ref[...]
    else:
        ext_scr[0:SUBLANES, :] = jnp.zeros((SUBLANES, dc), F32)
        pos = lax.broadcasted_iota(jnp.int32, (tm, 1), 0) % seg

    for c0 in range(0, dc, cb):
        cl = slice(c0, c0 + cb)
        cx = _dot(h, w_ref[:, dc + c0:dc + c0 + cb]) * _dot(h, w_ref[:, 2 * dc + c0:2 * dc + c0 + cb])
        ext_scr[SUBLANES:SUBLANES + tm, cl] = cx
        m1 = ext_scr[SUBLANES - 1:SUBLANES - 1 + tm, cl]
        m2 = ext_scr[SUBLANES - 2:SUBLANES - 2 + tm, cl]
        if seg is not None:
            m1 = jnp.where(pos == 0, h1_ref[:, cl], m1)
            m2 = jnp.where(pos == 0, h0_ref[:, cl], jnp.where(pos == 1, h1_ref[:, cl], m2))
            st_ref[:, cl] = cx
        uc = cw_ref[0:1, cl] * m2 + cw_ref[1:2, cl] * m1 + cw_ref[2:3, cl] * cx
        o_ref[:, cl] = (_dot(h, w_ref[:, c0:c0 + cb]) * uc).astype(o_ref.dtype)
    if seg is None:
        st_ref[...] = ext_scr[tm:tm + SUBLANES, :]
        if tiles_per_seq > 1:
            ext_scr[0:SUBLANES, :] = ext_scr[tm:tm + SUBLANES, :]


def _in_c(x, mod, tiles_per_group, tm, g, w, e, cw, h0, h1, tiles_per_seq, seg):
    r, d = x.shape
    dc = w.shape[2] // 3
    cb = 512 if dc % 512 == 0 else dc
    row = lambda width: pl.BlockSpec((tm, width), lambda i: (i, 0))
    if seg is None:
        nseq = r // (tm * tiles_per_seq)
        hspec = pl.BlockSpec((None, 1, dc), lambda i: (i // tiles_per_seq, 0, 0))
        st_spec = pl.BlockSpec((None, SUBLANES, dc), lambda i: (i // tiles_per_seq, 0, 0))
        st_shape = jax.ShapeDtypeStruct((nseq, SUBLANES, dc), F32)
    else:
        hspec = row(dc)
        st_spec = row(dc)
        st_shape = jax.ShapeDtypeStruct((r, dc), F32)
    return pl.pallas_call(
        functools.partial(_in_c_kernel, dc=dc, tm=tm, tiles_per_seq=tiles_per_seq, seg=seg, cb=cb),
        grid=(r // tm,),
        in_specs=[
            row(d),
            _mod_spec(mod, SH1, tiles_per_group, d),
            _mod_spec(mod, SC1, tiles_per_group, d),
            _resident((1, d)), _resident_layer(w, e), _resident(cw.shape), hspec, hspec,
        ],
        out_specs=[row(dc), st_spec],
        out_shape=[jax.ShapeDtypeStruct((r, dc), BF16), st_shape],
        scratch_shapes=[pltpu.VMEM((SUBLANES + tm, dc), F32)],
        compiler_params=_cparams(1),
        name="in_c",
    )(x, mod[0], mod[0], g, w, cw, h0, h1)


def _out_kernel(*refs, n_in):
    a_refs, w_refs = refs[:n_in], refs[n_in:2 * n_in]
    x_ref, gate_ref, sh_ref, sc_ref, g_ref, o_ref, h_ref = refs[2 * n_in:]
    y = _dot(a_refs[0][...], w_refs[0][...])
    for a_ref, w_ref in zip(a_refs[1:], w_refs[1:]):
        y = y + _dot(a_ref[...], w_ref[...])
    xn = x_ref[...] + gate_ref[...] * y
    o_ref[...] = xn
    h_ref[...] = _modulate(xn, g_ref[...], sh_ref[...], sc_ref[...]).astype(h_ref.dtype)


def _out_proj(acts, weights, e, x, mod, tiles_per_group, tm, g_ffn):
    r, d = x.shape
    row = lambda w: pl.BlockSpec((tm, w), lambda i: (i, 0))
    return pl.pallas_call(
        functools.partial(_out_kernel, n_in=len(acts)),
        grid=(r // tm,),
        in_specs=[row(a.shape[1]) for a in acts] + [_resident_layer(w, e) for w in weights]
        + [row(d), _mod_spec(mod, G1, tiles_per_group, d), _mod_spec(mod, SH2, tiles_per_group, d),
           _mod_spec(mod, SC2, tiles_per_group, d), _resident((1, d))],
        out_specs=[row(d), row(d)],
        out_shape=[jax.ShapeDtypeStruct((r, d), F32), jax.ShapeDtypeStruct((r, d), BF16)],
        compiler_params=_cparams(1),
        name="out_proj",
    )(*acts, *weights, x, mod[0], mod[0], mod[0], g_ffn)


def _ffn_kernel(h_ref, x_hbm, gate_ref, wg_ref, wu_ref, wd_ref, fg_ref, o_ref, x_buf, x_sem, *, final, tm):
    i, j = pl.program_id(0), pl.program_id(1)

    def x_copy():
        return pltpu.make_async_copy(x_hbm.at[pl.ds(pl.multiple_of(i * tm, tm), tm), :], x_buf, x_sem)

    @pl.when(j == 0)
    def _():
        x_copy().start()
        o_ref[...] = jnp.zeros(o_ref.shape, F32)

    h = h_ref[...]
    act = (_silu(_dot(h, wg_ref[...])) * _dot(h, wu_ref[...])).astype(BF16)
    o_ref[...] += _dot(act, wd_ref[...])

    @pl.when(j == pl.num_programs(1) - 1)
    def _():
        x_copy().wait()
        xn = x_buf[...] + gate_ref[...] * o_ref[...]
        if final:
            xn = _rms(xn) * fg_ref[...]
        o_ref[...] = xn


def _ffn(h, x, mod, tiles_per_group, tm, tf, wg, wu, wd, layer, final_g, final):
    r, d = x.shape
    f = wg.shape[2]
    row = pl.BlockSpec((tm, d), lambda i, j: (i, 0))
    return pl.pallas_call(
        functools.partial(_ffn_kernel, final=final, tm=tm),
        grid=(r // tm, f // tf),
        in_specs=[
            row, pl.BlockSpec(memory_space=pl.ANY),
            _mod_spec(mod, G2, tiles_per_group, d),
            pl.BlockSpec((None, d, tf), lambda i, j: (layer, 0, j)),
            pl.BlockSpec((None, d, tf), lambda i, j: (layer, 0, j)),
            pl.BlockSpec((None, tf, d), lambda i, j: (layer, j, 0)),
            _resident((1, d)),
        ],
        out_specs=row,
        out_shape=jax.ShapeDtypeStruct((r, d), F32),
        scratch_shapes=[pltpu.VMEM((tm, d), F32), pltpu.SemaphoreType.DMA(())],
        compiler_params=_cparams(2),
        name="ffn",
    )(h, x, mod[0], wg, wu, wd, final_g)


def _row_tile(t, cap):
    tm = min(t, cap)
    while t % tm:
        tm -= SUBLANES
    return tm


def _ff_tile(f, cap=512):
    tf = min(f, cap)
    while f % tf or tf % LANES:
        tf -= LANES
    return tf


def _trunk(x3, mods, p, cache):
    b, t, d = x3.shape
    depth = mods.shape[0]
    n_even = (depth + 1) // 2
    ha, dh, da, db = p["ha"], p["dh"], p["da"], p["db"]
    kb, kc = p["dw_b_w"].shape[1], p["conv_c_w"].shape[1]
    r = b * t
    x = x3.reshape(r, d)
    if cache is None:
        tm = _row_tile(t, 512)
        tiles_per_group = t // tm
        tiles_per_seq, seg = t // tm, None
        tm_ffn = _row_tile(t, 1024)
    else:
        tm = r
        tiles_per_group = 1
        tiles_per_seq, seg = 1, t
        tm_ffn = r
    kv = None
    fs, bs, cs = [], [], []
    for l in range(depth):
        mod = (mods, l, 1 if cache is None else r)
        e = l // 2
        g_ffn = p["norm_ffn_g"][l]
        if l % 2 == 0:
            q, k_all, v_all, u, lf_t = _in_ab(x, mod, tiles_per_group, tm, p["norm_mix_g"][l], p["wqkv"],
                                              p["wf"], p["bf"], p["wg_ab"], e, n_even, kv, dh)
            kv = (k_all, v_all)
            u3 = u.reshape(b, t, db)
            if cache is None:
                cum = _cumsum(lf_t, b, t)
                att = _attn_prompt(q, k_all, v_all, e, cum, b, t, ha, dh, _row_tile(t, 256))
                hist = jnp.zeros((b, kb - 1, db), F32)
            else:
                past = cache["k"].shape[2]
                npad = LANES
                lf_new = lf_t.reshape(ha, b, t).transpose(1, 0, 2)
                lf_all = jnp.concatenate([cache["logf"][e].transpose(0, 2, 1), lf_new,
                                          jnp.zeros((b, ha, npad - t), F32)], axis=-1)
                cum = _cumsum(lf_all.transpose(1, 0, 2).reshape(ha, b * (past + npad)), b, past + npad)
                pad = lambda a: jnp.pad(a[e].reshape(b, t, da), ((0, 0), (0, npad - t), (0, 0)))
                att = _attn_sample(q.reshape(b, t, da), pad(k_all), pad(v_all), cache["k"], cache["v"], e,
                                   cum).reshape(r, da)
                hist = cache["convb"][e]
            hist_pad = jnp.pad(hist, ((0, 0), (CONV_HALO - (kb - 1), 0), (0, 0)))
            conv_p = (p["dw_b_w"][e], p["dw_b_bias"][e], p["ln_b_g"][e], p["ln_b_b"][e])
            if cache is None:
                x, h = _mix_ab(att, u3, hist_pad, *conv_p, p["wout_a"], p["wout_b"], e, x, mod, tm, g_ffn)
            else:
                z = _convb(u3, hist_pad, *conv_p, _row_tile(t, 512)).reshape(r, db)
                x, h = _out_proj([att, z], [p["wout_a"], p["wout_b"]], e, x, mod, tiles_per_group, tm, g_ffn)
            fs.append(lf_t.reshape(ha, b, t).transpose(1, 2, 0))
            keep = kb - 1
            bs.append(u3[:, t - keep:] if t >= keep else jnp.concatenate([hist[:, t:], u3], axis=1))
        else:
            hist = jnp.zeros((b, kc - 1, d), F32) if cache is None else cache["convc"][e]
            if seg is None:
                h0, h1 = hist[:, 0:1], hist[:, 1:2]
            else:
                h0, h1 = jnp.repeat(hist[:, 0], t, axis=0), jnp.repeat(hist[:, 1], t, axis=0)
            gated, st = _in_c(x, mod, tiles_per_group, tm, p["norm_mix_g"][l], p["w_in_c"], e,
                              p["conv_c_w"][e], h0, h1, tiles_per_seq, seg)
            x, h = _out_proj([gated], [p["w_out_c"]], e, x, mod, tiles_per_group, tm, g_ffn)
            cs.append(st[:, -(kc - 1):] if seg is None else st.reshape(b, t, d)[:, -(kc - 1):])
        x = _ffn(h, x, mod, tiles_per_group * tm // tm_ffn, tm_ffn, _ff_tile(p["w_gate"].shape[2]),
                 p["w_gate"], p["w_up"], p["w_down"], l, p["final_g"], final=(l == depth - 1))
    k_all, v_all = kv
    return (x.reshape(b, t, d), k_all.reshape(n_even, b, t, ha, dh), v_all.reshape(n_even, b, t, ha, dh),
            jnp.stack(fs), jnp.stack(bs), jnp.stack(cs))


def kernel(x_prompt, x_sample, cache_k, cache_v, cache_logf, state_convb, state_convc, c_prompt, c_sample,
           ada_w, ada_b, norm_mix_g, norm_ffn_g, w_in_ab, b_f, dw_b_w, dw_b_bias, ln_b_g, ln_b_b, w_out_ab,
           w_in_c, conv_c_w, w_out_c, w_gate, w_up, w_down, final_g):
    bp, tp, d = x_prompt.shape
    bs_, ts, _ = x_sample.shape
    depth = ada_w.shape[0]
    ha = b_f.shape[1]
    d_in = w_in_ab.shape[2]
    da = d_in - ha - 2 * w_out_ab.shape[1]
    db = w_out_ab.shape[1] - da
    dh = da // ha
    assert conv_c_w.shape[1] == 3 and tp % SUBLANES == 0 and (bs_ * ts) % SUBLANES == 0

    bf = lambda a: a.astype(BF16)
    p = {
        "ha": ha, "dh": dh, "da": da, "db": db,
        "norm_mix_g": norm_mix_g.reshape(depth, 1, d), "norm_ffn_g": norm_ffn_g.reshape(depth, 1, d),
        "wqkv": bf(w_in_ab[:, :, 0:3 * da]),
        "wf": bf(w_in_ab[:, :, 3 * da:3 * da + ha].transpose(0, 2, 1)),
        "bf": b_f.reshape(-1, ha, 1),
        "wg_ab": bf(w_in_ab[:, :, 3 * da + ha:]),
        "dw_b_w": dw_b_w, "dw_b_bias": dw_b_bias.reshape(-1, 1, db),
        "ln_b_g": ln_b_g.reshape(-1, 1, db), "ln_b_b": ln_b_b.reshape(-1, 1, db),
        "wout_a": bf(w_out_ab[:, 0:da]), "wout_b": bf(w_out_ab[:, da:]),
        "w_in_c": bf(w_in_c), "conv_c_w": conv_c_w, "w_out_c": bf(w_out_c),
        "w_gate": bf(w_gate), "w_up": bf(w_up), "w_down": bf(w_down),
        "final_g": final_g.reshape(1, d),
    }

    rs = bs_ * ts
    nc = rs + bp
    nc_pad = -(-nc // SUBLANES) * SUBLANES
    c_all = jnp.pad(jnp.concatenate([jnp.repeat(c_sample, ts, axis=0), c_prompt], axis=0),
                    ((0, nc_pad - nc), (0, 0)))
    mod_all = _ada(c_all, ada_w, ada_b)
    mod_s = mod_all.reshape(depth, 1, nc_pad, 6 * d)
    mod_p = mod_all[:, rs:nc].reshape(depth, bp, 1, 6 * d)

    cache = {"k": cache_k, "v": cache_v, "logf": cache_logf, "convb": state_convb, "convc": state_convc}
    y_p, k_p, v_p, f_p, b_p, c_p = _trunk(x_prompt, mod_p, p, None)
    y_s, k_s, v_s, f_s, b_s, c_s = _trunk(x_sample, mod_s, p, cache)
    return (y_p, y_s, k_p, v_p, f_p, b_p, c_p, k_s, v_s, f_s, b_s, c_s)
```
